```python
import jax, jax.numpy as jnp
from jax import lax
import numpy as np

D_MODEL = 1024
BATCH = 16
SEQ = 2048
DEPTH = 2

GRID_W = 64
CTX_LEN = 256
RET_HEADS = 4
RET_HEAD_DIM = 64
RET_WIDTH = RET_HEADS * RET_HEAD_DIM
RET_CHUNK = 128
ROPE_BASE = 10000.0
CONV_WIDTH = D_MODEL // 4
CONV_KERNEL = 31
GMLP_WIDTH = D_MODEL // 4
GMLP_GROUPS = 4
GMLP_CHUNK = 128
FNET_WIDTH = D_MODEL // 4
FNET_GROUPS = 4
N_BRANCH = 4
D_FF = ((8 * D_MODEL // 3 + 127) // 128) * 128
FFN_KERNEL = 3
EPS = 1e-6

RET_Q = 0
RET_K = RET_Q + RET_WIDTH
RET_V = RET_K + RET_WIDTH
RET_G = RET_V + RET_WIDTH
CONF_OFF = RET_G + RET_WIDTH
GMLP_OFF = CONF_OFF + 2 * CONV_WIDTH
FNET_OFF = GMLP_OFF + 2 * GMLP_WIDTH
GATE_OFF = FNET_OFF + FNET_WIDTH
IN_COLS = GATE_OFF + N_BRANCH * D_MODEL

kernel_name = "hybrid_retention_conformer_gmlp_fnet_dit"


def rms_norm(x, gain):
    xf = x.astype(jnp.float32)
    y = xf * lax.rsqrt(jnp.mean(xf * xf, axis=-1, keepdims=True) + EPS)
    return (y * gain.astype(jnp.float32)).astype(x.dtype)


def layer_norm(x, gain, bias):
    xf = x.astype(jnp.float32)
    xc = xf - jnp.mean(xf, axis=-1, keepdims=True)
    y = xc * lax.rsqrt(jnp.mean(xc * xc, axis=-1, keepdims=True) + EPS)
    return (y * gain.astype(jnp.float32) + bias.astype(jnp.float32)).astype(x.dtype)


def modulate(x, gain, shift, scale):
    return rms_norm(x, gain) * (1 + scale) + shift


def split_heads(t):
    return t.reshape(t.shape[0], t.shape[1], RET_HEADS, RET_HEAD_DIM)


def rotary(t, pos):
    half = t.shape[-1] // 2
    inv_freq = ROPE_BASE ** (-jnp.arange(half, dtype=jnp.float32) / half)
    ang = pos.astype(jnp.float32)[:, None] * inv_freq[None, :]
    cos = jnp.cos(ang)[None, :, None, :]
    sin = jnp.sin(ang)[None, :, None, :]
    tf = t.astype(jnp.float32)
    t1, t2 = tf[..., :half], tf[..., half:]
    return jnp.concatenate([t1 * cos - t2 * sin, t1 * sin + t2 * cos], axis=-1).astype(t.dtype)


def retention_q(proj, pos):
    return rotary(split_heads(proj[..., RET_Q:RET_K]), pos)


def retention_kv(kv_cols, pos):
    k = rotary(split_heads(kv_cols[..., :RET_WIDTH]), pos) * RET_HEAD_DIM ** -0.5
    v = split_heads(kv_cols[..., RET_WIDTH:])
    return k, v


def retention_scan(q, k, v, log_g, state0):
    B_, L, H, d = q.shape
    n_chunks = L // RET_CHUNK
    qc = q.reshape(B_, n_chunks, RET_CHUNK, H, d)
    kc = k.reshape(B_, n_chunks, RET_CHUNK, H, d)
    vc = v.reshape(B_, n_chunks, RET_CHUNK, H, d)
    idx = jnp.arange(RET_CHUNK, dtype=jnp.float32)
    diff = idx[:, None] - idx[None, :]
    decay_mask = jnp.where(diff >= 0, jnp.exp(log_g[:, None, None] * jnp.maximum(diff, 0.0)), 0.0)
    scores = jnp.einsum('bnihd,bnjhd->bnhij', qc, kc) * decay_mask
    intra = jnp.einsum('bnhij,bnjhd->bnihd', scores, vc)
    w_kv = jnp.exp(log_g[:, None] * (RET_CHUNK - 1 - idx)[None, :])
    kv = jnp.einsum('bnjhd,hj,bnjhe->bnhde', kc, w_kv, vc)
    chunk_decay = jnp.exp(log_g * RET_CHUNK)[None, :, None, None]

    def step(s, kv_n):
        return chunk_decay * s + kv_n, s

    _, prev = lax.scan(step, state0, jnp.moveaxis(kv, 1, 0))
    prev = jnp.moveaxis(prev, 0, 1)
    w_q = jnp.exp(log_g[:, None] * (idx + 1.0)[None, :])
    cross = jnp.einsum('bnihd,bnhde,hi->bnihe', qc, prev, w_q)
    return (intra + cross).reshape(B_, L, H, d)


def retention_final_state(k, v, log_g):
    L = k.shape[1]
    w = jnp.exp(log_g[:, None] * (L - 1 - jnp.arange(L, dtype=jnp.float32))[None, :])
    return jnp.einsum('blhd,hl,blhe->bhde', k, w, v)


def head_norm(o, gain):
    B_, L, H, d = o.shape
    of = o.astype(jnp.float32)
    oc = of - jnp.mean(of, axis=-1, keepdims=True)
    y = oc * lax.rsqrt(jnp.mean(oc * oc, axis=-1, keepdims=True) + EPS)
    return y.reshape(B_, L, H * d) * gain.astype(jnp.float32)


def depthwise_conv(x, w, b):
    nd = w.ndim - 1
    dn = {1: ('NWC', 'WIO', 'NWC'), 2: ('NHWC', 'HWIO', 'NHWC')}[nd]
    pad = [((kw - 1) // 2, (kw - 1) // 2) for kw in w.shape[:-1]]
    y = lax.conv_general_dilated(x, w[..., None, :], (1,) * nd, pad,
                                 dimension_numbers=dn, feature_group_count=x.shape[-1])
    return y + b


def conformer_conv(a, w_dw, b_dw, ln_g, ln_b, w_o):
    h = a[..., :CONV_WIDTH] * jax.nn.sigmoid(a[..., CONV_WIDTH:])
    h = depthwise_conv(h, w_dw, b_dw)
    h = jax.nn.silu(layer_norm(h, ln_g, ln_b))
    return h @ w_o


def spatial_gating(z, ln_g, ln_b, w_s, b_s, w_o):
    z = jax.nn.gelu(z, approximate=False)
    u, v = z[..., :GMLP_WIDTH], z[..., GMLP_WIDTH:]
    v = layer_norm(v, ln_g, ln_b)
    B_, L, _ = v.shape
    vc = v.reshape(B_, L // GMLP_CHUNK, GMLP_CHUNK, GMLP_GROUPS, GMLP_WIDTH // GMLP_GROUPS)
    sv = jnp.einsum('gij,bnjgc->bnigc', w_s, vc) + b_s.T[:, :, None]
    return (u * sv.reshape(B_, L, GMLP_WIDTH)) @ w_o


def fourier_mix(f, w_o):
    B_, L, _ = f.shape
    fg = f.astype(jnp.float32).reshape(B_, L, FNET_GROUPS, FNET_WIDTH // FNET_GROUPS)
    m = jnp.fft.fftn(fg, axes=(1, 3), norm='ortho').real
    return m.reshape(B_, L, FNET_WIDTH).astype(f.dtype) @ w_o


def token_mixers(proj, q, k, v, s_f, s_b, log_g, lp):
    o = retention_scan(q, k, v, log_g[0], s_f)
    o = o + retention_scan(q[:, ::-1], k[:, ::-1], v[:, ::-1], log_g[1], s_b)[:, ::-1]
    o = head_norm(o, lp['ret_gn']).astype(proj.dtype)
    ret = (jax.nn.silu(proj[..., RET_G:CONF_OFF]) * o) @ lp['w_ret_o']
    conf = conformer_conv(proj[..., CONF_OFF:GMLP_OFF], lp['conv_dw'], lp['conv_db'],
                          lp['conv_ln_g'], lp['conv_ln_b'], lp['w_conv_o'])
    gm = spatial_gating(proj[..., GMLP_OFF:FNET_OFF], lp['gmlp_ln_g'], lp['gmlp_ln_b'],
                        lp['gmlp_ws'], lp['gmlp_bs'], lp['w_gmlp_o'])
    fn = fourier_mix(proj[..., FNET_OFF:GATE_OFF], lp['w_fnet_o'])
    g = jax.nn.sigmoid(proj[..., GATE_OFF:] + lp['b_gate'])
    merged = (g[..., 0:D_MODEL] * ret + g[..., D_MODEL:2 * D_MODEL] * conf
              + g[..., 2 * D_MODEL:3 * D_MODEL] * gm + g[..., 3 * D_MODEL:] * fn)
    return merged @ lp['w_out']


def conv_ffn(h, w_up, dw, db, w_down, on_grid):
    up = h @ w_up
    a, b = up[..., :D_FF], up[..., D_FF:]
    if on_grid:
        B_, L, _ = a.shape
        rows = L // GRID_W
        a = depthwise_conv(a.reshape(B_, rows, GRID_W, D_FF), dw, db).reshape(B_, L, D_FF)
    else:
        a = depthwise_conv(a, dw[FFN_KERNEL // 2], db)
    return (jax.nn.silu(a) * b) @ w_down


def setup_inputs(seed: int = 0) -> dict:
    key = jax.random.key(seed)
    keys = iter(jax.random.split(key, 40))
    f32 = jnp.float32

    def nrm(shape, scale):
        return scale * jax.random.normal(next(keys), shape, f32)

    def ones_noisy(shape):
        return 1.0 + nrm(shape, 0.02)

    gamma0 = 1.0 - 2.0 ** (-5.0 - jnp.arange(RET_HEADS, dtype=f32))
    logit0 = jnp.log(gamma0) - jnp.log1p(-gamma0)
    return {
        'x': nrm((BATCH, SEQ, D_MODEL), 1.0),
        'c': nrm((BATCH, D_MODEL), 1.0),
        'ctx': nrm((BATCH, CTX_LEN, D_MODEL), 1.0),
        'c_ctx': nrm((D_MODEL,), 1.0),
        'w_ada': nrm((DEPTH, D_MODEL, 6 * D_MODEL), 0.5 * D_MODEL ** -0.5),
        'b_ada': nrm((DEPTH, 6 * D_MODEL), 0.02),
        'g_norm1': ones_noisy((DEPTH, D_MODEL)),
        'g_norm2': ones_noisy((DEPTH, D_MODEL)),
        'w_in': nrm((DEPTH, D_MODEL, IN_COLS), D_MODEL ** -0.5),
        'b_gate': nrm((DEPTH, N_BRANCH * D_MODEL), 0.02),
        'ret_decay': logit0 + nrm((DEPTH, 2, RET_HEADS), 0.1),
        'ret_gn': ones_noisy((DEPTH, RET_WIDTH)),
        'w_ret_o': nrm((DEPTH, RET_WIDTH, D_MODEL), RET_WIDTH ** -0.5),
        'conv_dw': nrm((DEPTH, CONV_KERNEL, CONV_WIDTH), CONV_KERNEL ** -0.5),
        'conv_db': nrm((DEPTH, CONV_WIDTH), 0.02),
        'conv_ln_g': ones_noisy((DEPTH, CONV_WIDTH)),
        'conv_ln_b': nrm((DEPTH, CONV_WIDTH), 0.02),
        'w_conv_o': nrm((DEPTH, CONV_WIDTH, D_MODEL), CONV_WIDTH ** -0.5),
        'gmlp_ln_g': ones_noisy((DEPTH, GMLP_WIDTH)),
        'gmlp_ln_b': nrm((DEPTH, GMLP_WIDTH), 0.02),
        'gmlp_ws': nrm((DEPTH, GMLP_GROUPS, GMLP_CHUNK, GMLP_CHUNK), GMLP_CHUNK ** -0.5),
        'gmlp_bs': ones_noisy((DEPTH, GMLP_GROUPS, GMLP_CHUNK)),
        'w_gmlp_o': nrm((DEPTH, GMLP_WIDTH, D_MODEL), GMLP_WIDTH ** -0.5),
        'w_fnet_o': nrm((DEPTH, FNET_WIDTH, D_MODEL), FNET_WIDTH ** -0.5),
        'w_out': nrm((DEPTH, D_MODEL, D_MODEL), D_MODEL ** -0.5),
        'w_ffn_up': nrm((DEPTH, D_MODEL, 2 * D_FF), D_MODEL ** -0.5),
        'ffn_dw': nrm((DEPTH, FFN_KERNEL, FFN_KERNEL, D_FF), 1.0 / FFN_KERNEL),
        'ffn_db': nrm((DEPTH, D_FF), 0.02),
        'w_ffn_down': nrm((DEPTH, D_FF, D_MODEL), D_FF ** -0.5),
        'g_final': ones_noisy((D_MODEL,)),
    }


def reference(x, c, ctx, c_ctx, w_ada, b_ada, g_norm1, g_norm2, w_in, b_gate, ret_decay, ret_gn,
              w_ret_o, conv_dw, conv_db, conv_ln_g, conv_ln_b, w_conv_o, gmlp_ln_g, gmlp_ln_b,
              gmlp_ws, gmlp_bs, w_gmlp_o, w_fnet_o, w_out, w_ffn_up, ffn_dw, ffn_db, w_ffn_down,
              g_final):
    B_, L, _ = x.shape
    n_ctx = ctx.shape[1]
    pos_ctx = jnp.arange(n_ctx, dtype=jnp.int32)
    pos_lat = n_ctx + jnp.arange(L, dtype=jnp.int32)
    xc = ctx
    for l in range(DEPTH):
        last = l == DEPTH - 1
        sh1, sc1, ga1, sh2, sc2, ga2 = jnp.split(jax.nn.silu(c) @ w_ada[l] + b_ada[l], 6, axis=-1)
        csh1, csc1, cga1, csh2, csc2, cga2 = jnp.split(jax.nn.silu(c_ctx) @ w_ada[l] + b_ada[l], 6, axis=-1)
        log_g = jax.nn.log_sigmoid(ret_decay[l].astype(jnp.float32))
        lp = dict(ret_gn=ret_gn[l], w_ret_o=w_ret_o[l], conv_dw=conv_dw[l], conv_db=conv_db[l],
                  conv_ln_g=conv_ln_g[l], conv_ln_b=conv_ln_b[l], w_conv_o=w_conv_o[l],
                  gmlp_ln_g=gmlp_ln_g[l], gmlp_ln_b=gmlp_ln_b[l], gmlp_ws=gmlp_ws[l],
                  gmlp_bs=gmlp_bs[l], w_gmlp_o=w_gmlp_o[l], w_fnet_o=w_fnet_o[l],
                  b_gate=b_gate[l], w_out=w_out[l])

        hc = modulate(xc, g_norm1[l], csh1, csc1)
        if last:
            k_c, v_c = retention_kv(hc @ w_in[l, :, RET_K:RET_G], pos_ctx)
        else:
            proj_c = hc @ w_in[l]
            k_c, v_c = retention_kv(proj_c[..., RET_K:RET_G], pos_ctx)
            zero = jnp.zeros((B_, RET_HEADS, RET_HEAD_DIM, RET_HEAD_DIM), jnp.float32)
            mix_c = token_mixers(proj_c, retention_q(proj_c, pos_ctx), k_c, v_c, zero, zero, log_g, lp)
        s_f = retention_final_state(k_c, v_c, log_g[0])
        s_b = retention_final_state(k_c[:, ::-1], v_c[:, ::-1], log_g[1])

        hx = modulate(x, g_norm1[l], sh1[:, None], sc1[:, None])
        proj_x = hx @ w_in[l]
        k_x, v_x = retention_kv(proj_x[..., RET_K:RET_G], pos_lat)
        mix_x = token_mixers(proj_x, retention_q(proj_x, pos_lat), k_x, v_x, s_f, s_b, log_g, lp)
        x = x + ga1[:, None] * mix_x

        hx = modulate(x, g_norm2[l], sh2[:, None], sc2[:, None])
        x = x + ga2[:, None] * conv_ffn(hx, w_ffn_up[l], ffn_dw[l], ffn_db[l], w_ffn_down[l], True)
        if not last:
            xc = xc + cga1 * mix_c
            hc = modulate(xc, g_norm2[l], csh2, csc2)
            xc = xc + cga2 * conv_ffn(hc, w_ffn_up[l], ffn_dw[l], ffn_db[l], w_ffn_down[l], False)
    return rms_norm(x, g_final)
```

```python
import functools

import jax
import jax.numpy as jnp
import numpy as np
from jax import lax
from jax.experimental import pallas as pl
from jax.experimental.pallas import tpu as pltpu

D_MODEL = 1024
DEPTH = 2
GRID_W = 64
RET_HEADS = 4
RET_HEAD_DIM = 64
BW = 256
CHUNK = 128
ROPE_BASE = 10000.0
CONV_KERNEL = 31
GROUPS = 4
GROUP_C = BW // GROUPS
D_FF = ((8 * D_MODEL // 3 + 127) // 128) * 128
EPS = 1e-6
SMALL_COLS = 9 * BW
GATE_OFF = SMALL_COLS

F32 = jnp.float32
BF16 = jnp.bfloat16
HIGHEST = lax.Precision.HIGHEST
VMEM_LIMIT_BYTES = 56 * 1024 * 1024


def _cparams(*sem):
    return pltpu.CompilerParams(dimension_semantics=sem, vmem_limit_bytes=VMEM_LIMIT_BYTES)


def _dot(a, b):
    return jnp.dot(a, b, preferred_element_type=F32)


def _modnorm(x, gs, sh):
    ms = jnp.mean(x * x, axis=-1, keepdims=True)
    return x * lax.rsqrt(ms + EPS) * gs + sh


def _sigmoid(x):
    return 1.0 / (1.0 + jnp.exp(-x))


def _silu(x):
    return x * _sigmoid(x)


def _layer_norm(x, g, b):
    xc = x - jnp.mean(x, axis=-1, keepdims=True)
    return xc * lax.rsqrt(jnp.mean(xc * xc, axis=-1, keepdims=True) + EPS) * g + b


def _ada_kernel(c_ref, w_ref, b_ref, o_ref):
    o_ref[0] = jnp.dot(_silu(c_ref[...]), w_ref[0], precision=HIGHEST,
                       preferred_element_type=F32) + b_ref[0]


def _ada(cc, w_ada, b_ada):
    n, d = cc.shape
    cols = w_ada.shape[-1]
    tn = 1536
    return pl.pallas_call(
        _ada_kernel,
        grid=(DEPTH, cols // tn),
        in_specs=[pl.BlockSpec((n, d), lambda l, j: (0, 0)),
                  pl.BlockSpec((1, d, tn), lambda l, j: (l, 0, j)),
                  pl.BlockSpec((1, 1, tn), lambda l, j: (l, 0, j))],
        out_specs=pl.BlockSpec((1, n, tn), lambda l, j: (l, 0, j)),
        out_shape=jax.ShapeDtypeStruct((DEPTH, n, cols), F32),
        compiler_params=_cparams("parallel", "parallel"),
        name="ada",
    )(cc, w_ada, b_ada.reshape(DEPTH, 1, cols))


def _inproj_kernel(x_ref, gs_ref, sh_ref, w_ref, cos_ref, sin_ref,
                   qkvg_ref, conf_ref, gz_ref, ff_ref, hx_ref):
    hx_ref[...] = _modnorm(x_ref[0], gs_ref[0], sh_ref[0]).astype(BF16)
    cos = cos_ref[...]
    sin = sin_ref[...]
    half = BW // 2

    def rot(p, scale):
        p1, p2 = p[:, :half], p[:, half:]
        return jnp.concatenate([(p1 * cos - p2 * sin) * scale, (p1 * sin + p2 * cos) * scale], axis=-1)

    pq = _dot(hx_ref[...], w_ref[:, 0:BW])
    qkvg_ref[0, :, 0:BW] = rot(pq, 1.0).astype(BF16)
    pk = _dot(hx_ref[...], w_ref[:, BW:2 * BW])
    qkvg_ref[0, :, BW:2 * BW] = rot(pk, RET_HEAD_DIM ** -0.5).astype(BF16)
    pv = _dot(hx_ref[...], w_ref[:, 2 * BW:4 * BW])
    qkvg_ref[0, :, 2 * BW:4 * BW] = pv.astype(BF16)
    pc = _dot(hx_ref[...], w_ref[:, 4 * BW:6 * BW])
    conf_ref[0] = (pc[:, :BW] * _sigmoid(pc[:, BW:])).astype(BF16)
    gz_ref[0] = _dot(hx_ref[...], w_ref[:, 6 * BW:8 * BW]).astype(BF16)
    ff_ref[0] = _dot(hx_ref[...], w_ref[:, 8 * BW:9 * BW]).astype(BF16)


def _inproj(x, gs, sh, w_small, cos_t, sin_t):
    b, l, d = x.shape
    tm = min(l, 512)
    vec = pl.BlockSpec((1, 1, d), lambda i, j: (i, 0, 0))
    tab = pl.BlockSpec((tm, BW // 2), lambda i, j: (j, 0))

    def out(c):
        return pl.BlockSpec((1, tm, c), lambda i, j: (i, j, 0))

    return pl.pallas_call(
        _inproj_kernel,
        grid=(b, l // tm),
        in_specs=[pl.BlockSpec((1, tm, d), lambda i, j: (i, j, 0)), vec, vec,
                  pl.BlockSpec((d, SMALL_COLS), lambda i, j: (0, 0)), tab, tab],
        out_specs=[out(4 * BW), out(BW), out(2 * BW), out(BW)],
        out_shape=[jax.ShapeDtypeStruct((b, l, c), BF16) for c in (4 * BW, BW, 2 * BW, BW)],
        scratch_shapes=[pltpu.VMEM((tm, d), BF16)],
        compiler_params=_cparams("parallel", "parallel"),
        name="inproj",
    )(x, gs, sh, w_small, cos_t, sin_t)


def _ret_kernel(n_chunks, qkvg_ref, sf0_ref, sb0_ref, mst_ref, wqf_ref, wqb_ref, wkf_ref, wkb_ref,
                cdf_ref, cdb_ref, bd_ref, avg_ref, gn_ref, hm_ref,
                out_ref, sf_ref, sb_ref, o_scr, uf_scr, ub_scr, st_scr):
    unroll = 4 if n_chunks % 4 == 0 else (2 if n_chunks % 2 == 0 else 1)

    def chunk_rows(n):
        return pl.ds(pl.multiple_of(n * CHUNK, CHUNK), CHUNK)

    def state_update(k, wk_ref, v):
        kdec = (k * wk_ref[...]).astype(BF16)
        upd = lax.dot_general(kdec, v, (((0,), (0,)), ((), ())), preferred_element_type=F32)
        return bd_ref[...] * upd

    def intra(n, carry):
        rows = chunk_rows(n)
        q = qkvg_ref[0, rows, 0:BW].astype(F32)
        k = qkvg_ref[0, rows, BW:2 * BW]
        v = qkvg_ref[0, rows, 2 * BW:3 * BW]
        kf, vf = k.astype(F32), v.astype(F32)
        qst = jnp.concatenate([(q * hm_ref[h:h + 1, :]).astype(BF16) for h in range(RET_HEADS)], axis=0)
        s = lax.dot_general(qst, k, (((1,), (1,)), ((), ())), preferred_element_type=F32)
        p = (s * mst_ref[...]).astype(BF16)
        pcat = jnp.concatenate([p[h * CHUNK:(h + 1) * CHUNK] for h in range(RET_HEADS)], axis=1)
        vbd = jnp.concatenate([(vf * hm_ref[RET_HEADS + h:RET_HEADS + h + 1, :]).astype(BF16)
                               for h in range(RET_HEADS)], axis=0)
        o_scr[rows, :] = _dot(pcat, vbd)
        uf_scr[n] = state_update(kf, wkf_ref, v)
        ub_scr[n] = state_update(kf, wkb_ref, v)
        return carry

    lax.fori_loop(0, n_chunks, intra, 0, unroll=unroll)

    sf_ref[0] = sf0_ref[0]
    sb_ref[0] = sb0_ref[0]

    def scan(i, carry):
        st_scr[i, 0:BW, :] = sf_ref[0].astype(BF16)
        sf_ref[0] = cdf_ref[...] * sf_ref[0] + uf_scr[i]
        n = n_chunks - 1 - i
        st_scr[n, BW:2 * BW, :] = sb_ref[0].astype(BF16)
        sb_ref[0] = cdb_ref[...] * sb_ref[0] + ub_scr[n]
        return carry

    lax.fori_loop(0, n_chunks, scan, 0)

    def group_mean(t):
        hi = t.astype(BF16)
        lo = (t - hi.astype(F32)).astype(BF16)
        return _dot(jnp.concatenate([hi, lo], axis=1), avg_ref[...])

    def finish(n, carry):
        rows = chunk_rows(n)
        q = qkvg_ref[0, rows, 0:BW].astype(F32)
        qdec = jnp.concatenate([(q * wqf_ref[...]).astype(BF16), (q * wqb_ref[...]).astype(BF16)], axis=1)
        o = o_scr[rows, :] + _dot(qdec, st_scr[n])
        oc = o - group_mean(o)
        y = oc * lax.rsqrt(group_mean(oc * oc) + EPS) * gn_ref[...]
        g = qkvg_ref[0, rows, 3 * BW:4 * BW].astype(F32)
        out_ref[0, rows, :] = (_silu(g) * y).astype(BF16)
        return carry

    lax.fori_loop(0, n_chunks, finish, 0, unroll=unroll)


def _retention(qkvg, sf0, sb0, tabs, gn):
    b, l, _ = qkvg.shape
    n_chunks = l // CHUNK
    st = pl.BlockSpec((1, BW, BW), lambda i: (i, 0, 0))

    def full(a):
        return pl.BlockSpec(a.shape, lambda i: (0,) * a.ndim)

    consts = [tabs[k] for k in ("mst", "wqf", "wqb", "wkf", "wkb", "cdf", "cdb", "bd", "avg")] + [gn, tabs["hm"]]
    return pl.pallas_call(
        functools.partial(_ret_kernel, n_chunks),
        grid=(b,),
        in_specs=[pl.BlockSpec((1, l, 4 * BW), lambda i: (i, 0, 0)), st, st] + [full(a) for a in consts],
        out_specs=[pl.BlockSpec((1, l, BW), lambda i: (i, 0, 0)), st, st],
        out_shape=[jax.ShapeDtypeStruct((b, l, BW), BF16),
                   jax.ShapeDtypeStruct((b, BW, BW), F32),
                   jax.ShapeDtypeStruct((b, BW, BW), F32)],
        scratch_shapes=[pltpu.VMEM((l, BW), F32),
                        pltpu.VMEM((n_chunks, BW, BW), F32),
                        pltpu.VMEM((n_chunks, BW, BW), F32),
                        pltpu.VMEM((n_chunks, 2 * BW, BW), BF16)],
        compiler_params=_cparams("parallel"),
        name="retention",
    )(qkvg, sf0, sb0, *consts)


def _retention_tables(log_g):
    lf, lb = log_g[0], log_g[1]
    idx = jnp.arange(CHUNK, dtype=F32)
    diff = idx[:, None] - idx[None, :]
    m = (jnp.where(diff > 0, jnp.exp(lf[:, None, None] * jnp.maximum(diff, 0.0)), 0.0)
         + jnp.where(diff < 0, jnp.exp(lb[:, None, None] * jnp.maximum(-diff, 0.0)), 0.0)
         + jnp.where(diff == 0, 2.0, 0.0))
    lane = np.arange(BW)
    hq = (lane % (BW // 2)) // (RET_HEAD_DIM // 2)
    hv = lane // RET_HEAD_DIM
    lfq, lbq = lf[hq][None, :], lb[hq][None, :]
    i = idx[:, None]
    hm = np.concatenate([(hq[None, :] == np.arange(RET_HEADS)[:, None]),
                         (hv[None, :] == np.arange(RET_HEADS)[:, None])], axis=0).astype(np.float32)
    return dict(
        mst=m.reshape(RET_HEADS * CHUNK, CHUNK),
        wqf=jnp.exp(lfq * (i + 1.0)), wqb=jnp.exp(lbq * (CHUNK - i)),
        wkf=jnp.exp(lfq * (CHUNK - 1.0 - i)), wkb=jnp.exp(lbq * i),
        cdf=jnp.broadcast_to(jnp.exp(lf[hq] * CHUNK)[:, None], (BW, BW)),
        cdb=jnp.broadcast_to(jnp.exp(lb[hq] * CHUNK)[:, None], (BW, BW)),
        bd=jnp.asarray((hq[:, None] == hv[None, :]).astype(np.float32)),
        avg=jnp.asarray(np.tile((hv[:, None] == hv[None, :]).astype(np.float32) / RET_HEAD_DIM,
                                (2, 1))).astype(BF16),
        hm=jnp.asarray(hm),
    )


CONV_TILE = 64
CONV_PAD = 16


def _conf_kernel(l, h_ref, w_ref, b_ref, g_ref, be_ref, out_ref, hp_ref):
    zeros = jnp.zeros((CONV_PAD, BW), F32)
    hp_ref[0:CONV_PAD, :] = zeros
    hp_ref[CONV_PAD + l:2 * CONV_PAD + l, :] = zeros
    hp_ref[CONV_PAD:CONV_PAD + l, :] = h_ref[0].astype(F32)
    first = CONV_PAD - (CONV_KERNEL - 1) // 2

    def tile(t, carry):
        base = pl.multiple_of(t * CONV_TILE, CONV_TILE)
        win = hp_ref[pl.ds(base, CONV_TILE + 2 * CONV_PAD), :]
        acc = jnp.zeros((CONV_TILE, BW), F32) + b_ref[...]
        for s in range(8):
            part = None
            for m in range(4):
                k = 8 * m + s - first
                if 0 <= k < CONV_KERNEL:
                    term = win[8 * m:8 * m + CONV_TILE + 8, :] * w_ref[k:k + 1, :]
                    part = term if part is None else part + term
            if part is not None:
                acc = acc + part[s:s + CONV_TILE, :]
        y = _layer_norm(acc, g_ref[...], be_ref[...])
        out_ref[0, pl.ds(base, CONV_TILE), :] = _silu(y).astype(BF16)
        return carry

    lax.fori_loop(0, l // CONV_TILE, tile, 0, unroll=2)


def _conformer(h, w_dw, b_dw, ln_g, ln_b):
    b, l, _ = h.shape

    def full(a):
        return pl.BlockSpec(a.shape, lambda i: (0,) * a.ndim)

    consts = [w_dw, b_dw.reshape(1, BW), ln_g.reshape(1, BW), ln_b.reshape(1, BW)]
    return pl.pallas_call(
        functools.partial(_conf_kernel, l),
        grid=(b,),
        in_specs=[pl.BlockSpec((1, l, BW), lambda i: (i, 0, 0))] + [full(a) for a in consts],
        out_specs=pl.BlockSpec((1, l, BW), lambda i: (i, 0, 0)),
        out_shape=jax.ShapeDtypeStruct((b, l, BW), BF16),
        scratch_shapes=[pltpu.VMEM((l + 2 * CONV_PAD, BW), F32)],
        compiler_params=_cparams("parallel"),
        name="conformer",
    )(h, *consts)


def _gmlp_kernel(n_chunks, z_ref, g_ref, be_ref, ws_ref, bs_ref, gm_ref, out_ref):
    for n in range(n_chunks):
        rows = slice(n * CHUNK, (n + 1) * CHUNK)
        z = z_ref[0, rows, :].astype(F32)
        z = 0.5 * z * (1.0 + lax.erf(z * (2.0 ** -0.5)))
        u, v = z[:, :BW], z[:, BW:]
        v = _layer_norm(v, g_ref[...], be_ref[...]).astype(BF16)
        full = _dot(ws_ref[...], v)
        sv = bs_ref[...]
        for g in range(GROUPS):
            sv = sv + full[g * CHUNK:(g + 1) * CHUNK, :] * gm_ref[g:g + 1, :]
        out_ref[0, rows, :] = (u * sv).astype(BF16)


def _gmlp(z, ln_g, ln_b, ws, bs):
    b, l, _ = z.shape
    tm = min(l, 512)
    gm = np.repeat(np.eye(GROUPS, dtype=np.float32), GROUP_C, axis=1)
    consts = [ln_g.reshape(1, BW), ln_b.reshape(1, BW), ws.reshape(GROUPS * CHUNK, CHUNK).astype(BF16),
              jnp.repeat(bs.T, GROUP_C, axis=1), jnp.asarray(gm)]

    def full(a):
        return pl.BlockSpec(a.shape, lambda i, j: (0,) * a.ndim)

    return pl.pallas_call(
        functools.partial(_gmlp_kernel, tm // CHUNK),
        grid=(b, l // tm),
        in_specs=[pl.BlockSpec((1, tm, 2 * BW), lambda i, j: (i, j, 0))] + [full(a) for a in consts],
        out_specs=pl.BlockSpec((1, tm, BW), lambda i, j: (i, j, 0)),
        out_shape=jax.ShapeDtypeStruct((b, l, BW), BF16),
        compiler_params=_cparams("parallel", "parallel"),
        name="gmlp",
    )(z, *consts)


@functools.lru_cache(maxsize=None)
def _dft_tables(l):
    n = np.arange(l, dtype=np.int64)
    ang = 2.0 * np.pi * ((n[:, None] * n[None, :]) % l) / l
    cs = np.concatenate([np.cos(ang), -np.sin(ang)], axis=1)
    c = np.arange(GROUP_C, dtype=np.int64)
    angc = 2.0 * np.pi * ((c[:, None] * c[None, :]) % GROUP_C) / GROUP_C
    eye = np.eye(GROUPS)
    return (cs.astype(np.float32), np.kron(eye, np.cos(angc)).astype(np.float32),
            np.kron(eye, np.sin(angc)).astype(np.float32))


def _fnet_kernel(l, scale, f_ref, wc_ref, ws_ref, cs_ref, out_ref, xx_ref):
    @pl.when(pl.program_id(1) == 0)
    def _():
        step = min(l, 512)
        for t in range(l // step):
            rows = slice(t * step, (t + 1) * step)
            x = f_ref[0, rows, :]
            xx_ref[t * step:(t + 1) * step, :] = _dot(x, wc_ref[...]).astype(BF16)
            xx_ref[l + t * step:l + (t + 1) * step, :] = _dot(x, ws_ref[...]).astype(BF16)

    out_ref[0] = (_dot(cs_ref[...], xx_ref[...]) * scale).astype(BF16)


def _fnet(f):
    b, l, _ = f.shape
    cs, wc, ws = _dft_tables(l)
    tr = min(l, 512)
    scale = float(1.0 / np.sqrt(l * GROUP_C))
    return pl.pallas_call(
        functools.partial(_fnet_kernel, l, scale),
        grid=(b, l // tr),
        in_specs=[pl.BlockSpec((1, l, BW), lambda i, j: (i, 0, 0)),
                  pl.BlockSpec((BW, BW), lambda i, j: (0, 0)),
                  pl.BlockSpec((BW, BW), lambda i, j: (0, 0)),
                  pl.BlockSpec((tr, 2 * l), lambda i, j: (j, 0))],
        out_specs=pl.BlockSpec((1, tr, BW), lambda i, j: (i, j, 0)),
        out_shape=jax.ShapeDtypeStruct((b, l, BW), BF16),
        scratch_shapes=[pltpu.VMEM((2 * l, BW), BF16)],
        compiler_params=_cparams("parallel", "arbitrary"),
        name="fnet",
    )(f, jnp.asarray(wc).astype(BF16), jnp.asarray(ws).astype(BF16), jnp.asarray(cs).astype(BF16))


def _merge_kernel(x_ref, gs_ref, sh_ref, ga_ref, p0_ref, p1_ref, p2_ref, p3_ref,
                  wg_ref, bg_ref, wo_ref, wout_ref, out_ref, hx_ref, m_ref):
    x = x_ref[0]
    hx_ref[...] = _modnorm(x, gs_ref[0], sh_ref[0]).astype(BF16)
    for i, p_ref in enumerate((p0_ref, p1_ref, p2_ref, p3_ref)):
        cols = slice(i * D_MODEL, (i + 1) * D_MODEL)
        gate = _sigmoid(_dot(hx_ref[...], wg_ref[:, cols]) + bg_ref[:, cols])
        term = gate * _dot(p_ref[0], wo_ref[i])
        if i == 0:
            m_ref[...] = term
        else:
            m_ref[...] += term
    out_ref[0] = x + ga_ref[0] * _dot(m_ref[...].astype(BF16), wout_ref[...])


def _merge(x, gs, sh, ga, pres, w_gate, b_gate, w_bo, w_out):
    b, l, d = x.shape
    tm = min(l, 512)
    vec = pl.BlockSpec((1, 1, d), lambda i, j: (i, 0, 0))
    pre = pl.BlockSpec((1, tm, BW), lambda i, j: (i, j, 0))
    return pl.pallas_call(
        _merge_kernel,
        grid=(b, l // tm),
        in_specs=[pl.BlockSpec((1, tm, d), lambda i, j: (i, j, 0)), vec, vec, vec, pre, pre, pre, pre,
                  pl.BlockSpec((d, 4 * d), lambda i, j: (0, 0)),
                  pl.BlockSpec((1, 4 * d), lambda i, j: (0, 0)),
                  pl.BlockSpec((4, BW, d), lambda i, j: (0, 0, 0)),
                  pl.BlockSpec((d, d), lambda i, j: (0, 0))],
        out_specs=pl.BlockSpec((1, tm, d), lambda i, j: (i, j, 0)),
        out_shape=jax.ShapeDtypeStruct((b, l, d), F32),
        scratch_shapes=[pltpu.VMEM((tm, d), BF16), pltpu.VMEM((tm, d), F32)],
        compiler_params=_cparams("parallel", "parallel"),
        name="merge",
    )(x, gs, sh, ga, *pres, w_gate, b_gate.reshape(1, 4 * d), w_bo, w_out)


FFN_TC = 256
FFN_ROWS = 256
FFN_SUB = 128


def _ffn_pad(grid_w, row_conv):
    return (grid_w if row_conv else 0) + 8


def _ffn_kernel(l, grid_w, row_conv, final_norm, x_ref, gs_ref, sh_ref, ga_ref, wa_ref, wb_ref, dw_ref,
                db_ref, wd_ref, gf_ref, out_ref, hx_ref, ac_ref, al_ref, ar_ref, b_ref, h_ref):
    j = pl.program_id(1)
    pad = _ffn_pad(grid_w, row_conv)
    n_tiles = l // FFN_ROWS

    @pl.when(j == 0)
    def _():
        for t in range(n_tiles):
            rows = slice(t * FFN_ROWS, (t + 1) * FFN_ROWS)
            hx_ref[rows, :] = _modnorm(x_ref[0, rows, :], gs_ref[0], sh_ref[0]).astype(BF16)
        for ref in (ac_ref, al_ref, ar_ref):
            ref[0:pad, :] = jnp.zeros((pad, FFN_TC), F32)
            ref[pad + l:2 * pad + l, :] = jnp.zeros((pad, FFN_TC), F32)
        out_ref[0] = jnp.zeros((l, D_MODEL), F32)

    col = lax.broadcasted_iota(jnp.int32, (FFN_ROWS, FFN_TC), 0) % grid_w
    not_first = col != 0
    not_last = col != grid_w - 1

    step = min(l, 512)
    for t in range(l // step):
        rows = slice(t * step, (t + 1) * step)
        ac_ref[pad + t * step:pad + (t + 1) * step, :] = _dot(hx_ref[rows, :], wa_ref[...])
        b_ref[rows, :] = _dot(hx_ref[rows, :], wb_ref[...])

    for t in range(n_tiles):
        lo = pad + t * FFN_ROWS
        al_ref[lo:lo + FFN_ROWS, :] = jnp.where(not_first, ac_ref[lo - 1:lo - 1 + FFN_ROWS, :], 0.0)
        ar_ref[lo:lo + FFN_ROWS, :] = jnp.where(not_last, ac_ref[lo + 1:lo + 1 + FFN_ROWS, :], 0.0)

    for t in range(l // FFN_SUB):
        acc = jnp.zeros((FFN_SUB, FFN_TC), F32) + db_ref[...]
        for dr in ((-1, 0, 1) if row_conv else (0,)):
            off = pad + t * FFN_SUB + dr * grid_w
            tap = 3 * (dr + 1)
            acc = acc + al_ref[off:off + FFN_SUB, :] * dw_ref[tap:tap + 1, :]
            acc = acc + ac_ref[off:off + FFN_SUB, :] * dw_ref[tap + 1:tap + 2, :]
            acc = acc + ar_ref[off:off + FFN_SUB, :] * dw_ref[tap + 2:tap + 3, :]
        rows = slice(t * FFN_SUB, (t + 1) * FFN_SUB)
        h_ref[rows, :] = (_silu(acc) * b_ref[rows, :]).astype(BF16)

    for t in range(l // step):
        rows = slice(t * step, (t + 1) * step)
        out_ref[0, rows, :] += _dot(h_ref[rows, :], wd_ref[...])

    @pl.when(j == pl.num_programs(1) - 1)
    def _():
        for t in range(l // FFN_ROWS):
            rows = slice(t * FFN_ROWS, (t + 1) * FFN_ROWS)
            y = x_ref[0, rows, :] + ga_ref[0] * out_ref[0, rows, :]
            if final_norm:
                y = y * lax.rsqrt(jnp.mean(y * y, axis=-1, keepdims=True) + EPS) * gf_ref[...]
            out_ref[0, rows, :] = y


def _conv_ffn(x, gs, sh, ga, w_up, dw, db, w_down, g_final, grid_w, row_conv, final_norm):
    b, l, d = x.shape
    nj = D_FF // FFN_TC
    pad = _ffn_pad(grid_w, row_conv)
    vec = pl.BlockSpec((1, 1, d), lambda i, j: (i, 0, 0))
    return pl.pallas_call(
        functools.partial(_ffn_kernel, l, grid_w, row_conv, final_norm),
        grid=(b, nj),
        in_specs=[pl.BlockSpec((1, l, d), lambda i, j: (i, 0, 0)), vec, vec, vec,
                  pl.BlockSpec((d, FFN_TC), lambda i, j: (0, j)),
                  pl.BlockSpec((d, FFN_TC), lambda i, j: (0, j + nj)),
                  pl.BlockSpec((9, FFN_TC), lambda i, j: (0, j)),
                  pl.BlockSpec((1, FFN_TC), lambda i, j: (0, j)),
                  pl.BlockSpec((FFN_TC, d), lambda i, j: (j, 0)),
                  pl.BlockSpec((1, d), lambda i, j: (0, 0))],
        out_specs=pl.BlockSpec((1, l, d), lambda i, j: (i, 0, 0)),
        out_shape=jax.ShapeDtypeStruct((b, l, d), F32),
        scratch_shapes=([pltpu.VMEM((l, d), BF16)] + [pltpu.VMEM((l + 2 * pad, FFN_TC), F32)] * 3
                        + [pltpu.VMEM((l, FFN_TC), F32), pltpu.VMEM((l, FFN_TC), BF16)]),
        compiler_params=_cparams("parallel", "arbitrary"),
        name="conv_ffn",
    )(x, gs, sh, ga, w_up, w_up, dw.reshape(9, D_FF), db.reshape(1, D_FF), w_down, g_final.reshape(1, d))


def _rope_tables(n_pos):
    half = RET_HEAD_DIM // 2
    inv_freq = ROPE_BASE ** (-jnp.arange(half, dtype=F32) / half)
    ang = jnp.arange(n_pos, dtype=jnp.int32).astype(F32)[:, None] * inv_freq[None, :]
    return jnp.tile(jnp.cos(ang), (1, RET_HEADS)), jnp.tile(jnp.sin(ang), (1, RET_HEADS))


def _split_half_perm():
    half = RET_HEAD_DIM // 2
    t, h, i = np.meshgrid(np.arange(2), np.arange(RET_HEADS), np.arange(half), indexing="ij")
    return (h * RET_HEAD_DIM + t * half + i).reshape(-1)


def kernel(x, c, ctx, c_ctx, w_ada, b_ada, g_norm1, g_norm2, w_in, b_gate, ret_decay, ret_gn, w_ret_o, conv_dw, conv_db, conv_ln_g, conv_ln_b, w_conv_o, gmlp_ln_g, gmlp_ln_b, gmlp_ws, gmlp_bs, w_gmlp_o, w_fnet_o, w_out, w_ffn_up, ffn_dw, ffn_db, w_ffn_down, g_final):
    bsz, seq, d = x.shape
    n_ctx = ctx.shape[1]
    cc = jnp.concatenate([c, c_ctx[None, :], jnp.zeros((7, d), F32)], axis=0)
    mods = _ada(cc, w_ada, b_ada)
    cos_t, sin_t = _rope_tables(n_ctx + seq)
    perm = _split_half_perm()
    zero_state = jnp.zeros((bsz, BW, BW), F32)
    xc = ctx

    def mixers(stream, l_idx, gs, sh, ga, cos_p, sin_p, sf0, sb0, tabs, want_out):
        w = w_in[l_idx]
        w_small = jnp.concatenate([w[:, :BW][:, perm], w[:, BW:2 * BW][:, perm], w[:, 2 * BW:SMALL_COLS]],
                                  axis=1).astype(BF16)
        qkvg, conf_h, gz, ff = _inproj(stream, gs, sh, w_small, cos_p, sin_p)
        ret_pre, sf, sb = _retention(qkvg, sf0, sb0, tabs, ret_gn[l_idx].reshape(1, BW))
        if not want_out:
            return None, sf, sb
        conf_pre = _conformer(conf_h, conv_dw[l_idx], conv_db[l_idx], conv_ln_g[l_idx], conv_ln_b[l_idx])
        gm_pre = _gmlp(gz, gmlp_ln_g[l_idx], gmlp_ln_b[l_idx], gmlp_ws[l_idx], gmlp_bs[l_idx])
        fn_pre = _fnet(ff)
        w_bo = jnp.stack([w_ret_o[l_idx], w_conv_o[l_idx], w_gmlp_o[l_idx], w_fnet_o[l_idx]]).astype(BF16)
        new = _merge(stream, gs, sh, ga, (ret_pre, conf_pre, gm_pre, fn_pre),
                     w[:, GATE_OFF:].astype(BF16), b_gate[l_idx], w_bo, w_out[l_idx].astype(BF16))
        return new, sf, sb

    for l_idx in range(DEPTH):
        last = l_idx == DEPTH - 1
        lat = [m[:, None, :] for m in jnp.split(mods[l_idx, :bsz], 6, axis=-1)]
        cm = [jnp.broadcast_to(m[None, None, :], (bsz, 1, d)) for m in jnp.split(mods[l_idx, bsz], 6, axis=-1)]
        g1 = g_norm1[l_idx][None, None, :]
        g2 = g_norm2[l_idx][None, None, :]
        tabs = _retention_tables(jax.nn.log_sigmoid(ret_decay[l_idx].astype(F32)))
        w_up = w_ffn_up[l_idx].astype(BF16)
        w_down = w_ffn_down[l_idx].astype(BF16)

        xc_mix, s_f, s_b = mixers(xc, l_idx, g1 * (1 + cm[1]), cm[0], cm[2], cos_t[:n_ctx], sin_t[:n_ctx],
                                  zero_state, zero_state, tabs, not last)
        x, _, _ = mixers(x, l_idx, g1 * (1 + lat[1]), lat[0], lat[2], cos_t[n_ctx:], sin_t[n_ctx:],
                         s_f, s_b, tabs, True)
        x = _conv_ffn(x, g2 * (1 + lat[4]), lat[3], lat[5], w_up, ffn_dw[l_idx], ffn_db[l_idx], w_down,
                      g_final, GRID_W, True, last)
        if not last:
            fold = max(1, seq // n_ctx)
            while bsz % fold:
                fold //= 2
            xc = _conv_ffn(xc_mix.reshape(bsz // fold, fold * n_ctx, d), (g2 * (1 + cm[4]))[:bsz // fold],
                           cm[3][:bsz // fold], cm[5][:bsz // fold], w_up, ffn_dw[l_idx], ffn_db[l_idx],
                           w_down, g_final, n_ctx, False, False).reshape(bsz, n_ctx, d)
    return x
```

```python
import functools

import jax
import jax.numpy as jnp
import numpy as np
from jax import lax
from jax.experimental import pallas as pl
from jax.experimental.pallas import tpu as pltpu

D_MODEL = 1024
DEPTH = 2
GRID_W = 64
RET_HEADS = 4
RET_HEAD_DIM = 64
BW = 256
CHUNK = 128
ROPE_BASE = 10000.0
CONV_KERNEL = 31
GROUPS = 4
GROUP_C = BW // GROUPS
D_FF = ((8 * D_MODEL // 3 + 127) // 128) * 128
EPS = 1e-6
SMALL_COLS = 9 * BW
GATE_OFF = SMALL_COLS

F32 = jnp.float32
BF16 = jnp.bfloat16
HIGHEST = lax.Precision.HIGHEST
VMEM_LIMIT_BYTES = 56 * 1024 * 1024


def _cparams(*sem):
    return pltpu.CompilerParams(dimension_semantics=sem, vmem_limit_bytes=VMEM_LIMIT_BYTES)


def _dot(a, b):
    return jnp.dot(a, b, preferred_element_type=F32)


def _modnorm(x, gs, sh):
    ms = jnp.mean(x * x, axis=-1, keepdims=True)
    return x * lax.rsqrt(ms + EPS) * gs + sh


def _sigmoid(x):
    return 1.0 / (1.0 + jnp.exp(-x))


def _silu(x):
    return x * _sigmoid(x)


def _layer_norm(x, g, b):
    xc = x - jnp.mean(x, axis=-1, keepdims=True)
    return xc * lax.rsqrt(jnp.mean(xc * xc, axis=-1, keepdims=True) + EPS) * g + b


def _ada_kernel(c_ref, w_ref, b_ref, o_ref):
    o_ref[0] = jnp.dot(_silu(c_ref[...]), w_ref[0], precision=HIGHEST,
                       preferred_element_type=F32) + b_ref[0]


def _ada(cc, w_ada, b_ada):
    n, d = cc.shape
    cols = w_ada.shape[-1]
    tn = 1536
    return pl.pallas_call(
        _ada_kernel,
        grid=(DEPTH, cols // tn),
        in_specs=[pl.BlockSpec((n, d), lambda l, j: (0, 0)),
                  pl.BlockSpec((1, d, tn), lambda l, j: (l, 0, j)),
                  pl.BlockSpec((1, 1, tn), lambda l, j: (l, 0, j))],
        out_specs=pl.BlockSpec((1, n, tn), lambda l, j: (l, 0, j)),
        out_shape=jax.ShapeDtypeStruct((DEPTH, n, cols), F32),
        compiler_params=_cparams("parallel", "parallel"),
        name="ada",
    )(cc, w_ada, b_ada.reshape(DEPTH, 1, cols))


def _inproj_kernel(x_ref, gs_ref, sh_ref, w_ref, cos_ref, sin_ref,
                   qkvg_ref, conf_ref, gz_ref, ff_ref, hx_ref):
    hx_ref[...] = _modnorm(x_ref[0], gs_ref[0], sh_ref[0]).astype(BF16)
    cos = cos_ref[...]
    sin = sin_ref[...]
    half = BW // 2

    def rot(p, scale):
        p1, p2 = p[:, :half], p[:, half:]
        return jnp.concatenate([(p1 * cos - p2 * sin) * scale, (p1 * sin + p2 * cos) * scale], axis=-1)

    pq = _dot(hx_ref[...], w_ref[:, 0:BW])
    qkvg_ref[0, :, 0:BW] = rot(pq, 1.0).astype(BF16)
    pk = _dot(hx_ref[...], w_ref[:, BW:2 * BW])
    qkvg_ref[0, :, BW:2 * BW] = rot(pk, RET_HEAD_DIM ** -0.5).astype(BF16)
    pv = _dot(hx_ref[...], w_ref[:, 2 * BW:4 * BW])
    qkvg_ref[0, :, 2 * BW:4 * BW] = pv.astype(BF16)
    pc = _dot(hx_ref[...], w_ref[:, 4 * BW:6 * BW])
    conf_ref[0] = (pc[:, :BW] * _sigmoid(pc[:, BW:])).astype(BF16)
    gz_ref[0] = _dot(hx_ref[...], w_ref[:, 6 * BW:8 * BW]).astype(BF16)
    ff_ref[0] = _dot(hx_ref[...], w_ref[:, 8 * BW:9 * BW]).astype(BF16)


def _inproj(x, gs, sh, w_small, cos_t, sin_t):
    b, l, d = x.shape
    tm = min(l, 512)
    vec = pl.BlockSpec((1, 1, d), lambda i, j: (i, 0, 0))
    tab = pl.BlockSpec((tm, BW // 2), lambda i, j: (j, 0))

    def out(c):
        return pl.BlockSpec((1, tm, c), lambda i, j: (i, j, 0))

    return pl.pallas_call(
        _inproj_kernel,
        grid=(b, l // tm),
        in_specs=[pl.BlockSpec((1, tm, d), lambda i, j: (i, j, 0)), vec, vec,
                  pl.BlockSpec((d, SMALL_COLS), lambda i, j: (0, 0)), tab, tab],
        out_specs=[out(4 * BW), out(BW), out(2 * BW), out(BW)],
        out_shape=[jax.ShapeDtypeStruct((b, l, c), BF16) for c in (4 * BW, BW, 2 * BW, BW)],
        scratch_shapes=[pltpu.VMEM((tm, d), BF16)],
        compiler_params=_cparams("parallel", "parallel"),
        name="inproj",
    )(x, gs, sh, w_small, cos_t, sin_t)


def _mixer_kernel(n_chunks, qkvg_ref, conf_ref, gz_ref, sf0_ref, sb0_ref,
                  mst_ref, wqf_ref, wqb_ref, wkf_ref, wkb_ref, cdf_ref, cdb_ref, bd_ref, avg_ref, gn_ref, hm_ref,
                  cw_ref, cb_ref, cg_ref, cbe_ref, mg_ref, mbe_ref, mws_ref, mbs_ref, mgm_ref,
                  out_ref, cout_ref, mout_ref, sf_ref, sb_ref, o_scr, uf_scr, ub_scr, st_scr, hp_scr):
    unroll = 4 if n_chunks % 4 == 0 else (2 if n_chunks % 2 == 0 else 1)
    l = n_chunks * CHUNK
    hp_scr[0:CONV_PAD, :] = jnp.zeros((CONV_PAD, BW), F32)
    hp_scr[CONV_PAD + l:2 * CONV_PAD + l, :] = jnp.zeros((CONV_PAD, BW), F32)

    def chunk_rows(n):
        return pl.ds(pl.multiple_of(n * CHUNK, CHUNK), CHUNK)

    def state_update(k, wk_ref, v):
        kdec = (k * wk_ref[...]).astype(BF16)
        upd = lax.dot_general(kdec, v, (((0,), (0,)), ((), ())), preferred_element_type=F32)
        return bd_ref[...] * upd

    def intra(n, carry):
        rows = chunk_rows(n)
        q = qkvg_ref[0, rows, 0:BW].astype(F32)
        k = qkvg_ref[0, rows, BW:2 * BW]
        v = qkvg_ref[0, rows, 2 * BW:3 * BW]
        kf, vf = k.astype(F32), v.astype(F32)
        qst = jnp.concatenate([(q * hm_ref[h:h + 1, :]).astype(BF16) for h in range(RET_HEADS)], axis=0)
        s = lax.dot_general(qst, k, (((1,), (1,)), ((), ())), preferred_element_type=F32)
        p = (s * mst_ref[...]).astype(BF16)
        pcat = jnp.concatenate([p[h * CHUNK:(h + 1) * CHUNK] for h in range(RET_HEADS)], axis=1)
        vbd = jnp.concatenate([(vf * hm_ref[RET_HEADS + h:RET_HEADS + h + 1, :]).astype(BF16)
                               for h in range(RET_HEADS)], axis=0)
        o_scr[rows, :] = _dot(pcat, vbd)
        uf_scr[n] = state_update(kf, wkf_ref, v)
        ub_scr[n] = state_update(kf, wkb_ref, v)
        mout_ref[0, rows, :] = _gmlp_chunk(gz_ref[0, rows, :], mg_ref, mbe_ref, mws_ref, mbs_ref,
                                           mgm_ref).astype(BF16)
        hp_scr[pl.ds(pl.multiple_of(n * CHUNK + CONV_PAD, 8), CHUNK), :] = conf_ref[0, rows, :].astype(F32)
        return carry

    lax.fori_loop(0, n_chunks, intra, 0, unroll=unroll)

    sf_ref[0] = sf0_ref[0]
    sb_ref[0] = sb0_ref[0]

    def scan(i, carry):
        st_scr[i, 0:BW, :] = sf_ref[0].astype(BF16)
        sf_ref[0] = cdf_ref[...] * sf_ref[0] + uf_scr[i]
        n = n_chunks - 1 - i
        st_scr[n, BW:2 * BW, :] = sb_ref[0].astype(BF16)
        sb_ref[0] = cdb_ref[...] * sb_ref[0] + ub_scr[n]
        return carry

    lax.fori_loop(0, n_chunks, scan, 0)

    def group_mean(t):
        hi = t.astype(BF16)
        lo = (t - hi.astype(F32)).astype(BF16)
        return _dot(jnp.concatenate([hi, lo], axis=1), avg_ref[...])

    def finish(n, carry):
        rows = chunk_rows(n)
        q = qkvg_ref[0, rows, 0:BW].astype(F32)
        qdec = jnp.concatenate([(q * wqf_ref[...]).astype(BF16), (q * wqb_ref[...]).astype(BF16)], axis=1)
        o = o_scr[rows, :] + _dot(qdec, st_scr[n])
        oc = o - group_mean(o)
        y = oc * lax.rsqrt(group_mean(oc * oc) + EPS) * gn_ref[...]
        g = qkvg_ref[0, rows, 3 * BW:4 * BW].astype(F32)
        out_ref[0, rows, :] = (_silu(g) * y).astype(BF16)
        for t in range(CHUNK // CONV_TILE):
            base = pl.multiple_of(n * CHUNK + t * CONV_TILE, CONV_TILE)
            cout_ref[0, pl.ds(base, CONV_TILE), :] = _conv_tile(hp_scr, base, cw_ref, cb_ref, cg_ref,
                                                                cbe_ref).astype(BF16)
        return carry

    lax.fori_loop(0, n_chunks, finish, 0, unroll=unroll)


def _mixers(qkvg, conf_h, gz, sf0, sb0, tabs, gn, conv_consts, gmlp_consts):
    b, l, _ = qkvg.shape
    n_chunks = l // CHUNK
    st = pl.BlockSpec((1, BW, BW), lambda i: (i, 0, 0))

    def seq(c):
        return pl.BlockSpec((1, l, c), lambda i: (i, 0, 0))

    def full(a):
        return pl.BlockSpec(a.shape, lambda i: (0,) * a.ndim)

    consts = ([tabs[k] for k in ("mst", "wqf", "wqb", "wkf", "wkb", "cdf", "cdb", "bd", "avg")] + [gn, tabs["hm"]]
              + conv_consts + gmlp_consts)
    return pl.pallas_call(
        functools.partial(_mixer_kernel, n_chunks),
        grid=(b,),
        in_specs=[seq(4 * BW), seq(BW), seq(2 * BW), st, st] + [full(a) for a in consts],
        out_specs=[seq(BW), seq(BW), seq(BW), st, st],
        out_shape=[jax.ShapeDtypeStruct((b, l, BW), BF16)] * 3 + [jax.ShapeDtypeStruct((b, BW, BW), F32)] * 2,
        scratch_shapes=[pltpu.VMEM((l, BW), F32),
                        pltpu.VMEM((n_chunks, BW, BW), F32),
                        pltpu.VMEM((n_chunks, BW, BW), F32),
                        pltpu.VMEM((n_chunks, 2 * BW, BW), BF16),
                        pltpu.VMEM((l + 2 * CONV_PAD, BW), F32)],
        compiler_params=_cparams("parallel"),
        name="mixers",
    )(qkvg, conf_h, gz, sf0, sb0, *consts)


def _retention_tables(log_g):
    lf, lb = log_g[0], log_g[1]
    idx = jnp.arange(CHUNK, dtype=F32)
    diff = idx[:, None] - idx[None, :]
    m = (jnp.where(diff > 0, jnp.exp(lf[:, None, None] * jnp.maximum(diff, 0.0)), 0.0)
         + jnp.where(diff < 0, jnp.exp(lb[:, None, None] * jnp.maximum(-diff, 0.0)), 0.0)
         + jnp.where(diff == 0, 2.0, 0.0))
    lane = np.arange(BW)
    hq = (lane % (BW // 2)) // (RET_HEAD_DIM // 2)
    hv = lane // RET_HEAD_DIM
    lfq, lbq = lf[hq][None, :], lb[hq][None, :]
    i = idx[:, None]
    hm = np.concatenate([(hq[None, :] == np.arange(RET_HEADS)[:, None]),
                         (hv[None, :] == np.arange(RET_HEADS)[:, None])], axis=0).astype(np.float32)
    return dict(
        mst=m.reshape(RET_HEADS * CHUNK, CHUNK),
        wqf=jnp.exp(lfq * (i + 1.0)), wqb=jnp.exp(lbq * (CHUNK - i)),
        wkf=jnp.exp(lfq * (CHUNK - 1.0 - i)), wkb=jnp.exp(lbq * i),
        cdf=jnp.broadcast_to(jnp.exp(lf[hq] * CHUNK)[:, None], (BW, BW)),
        cdb=jnp.broadcast_to(jnp.exp(lb[hq] * CHUNK)[:, None], (BW, BW)),
        bd=jnp.asarray((hq[:, None] == hv[None, :]).astype(np.float32)),
        avg=jnp.asarray(np.tile((hv[:, None] == hv[None, :]).astype(np.float32) / RET_HEAD_DIM,
                                (2, 1))).astype(BF16),
        hm=jnp.asarray(hm),
    )


CONV_TILE = 64
CONV_PAD = 16


def _conv_tile(hp_ref, base, w_ref, b_ref, g_ref, be_ref):
    first = CONV_PAD - (CONV_KERNEL - 1) // 2
    win = hp_ref[pl.ds(base, CONV_TILE + 2 * CONV_PAD), :]
    acc = jnp.zeros((CONV_TILE, BW), F32) + b_ref[...]
    for s in range(8):
        part = None
        for m in range(4):
            k = 8 * m + s - first
            if 0 <= k < CONV_KERNEL:
                term = win[8 * m:8 * m + CONV_TILE + 8, :] * w_ref[k:k + 1, :]
                part = term if part is None else part + term
        if part is not None:
            acc = acc + part[s:s + CONV_TILE, :]
    return _silu(_layer_norm(acc, g_ref[...], be_ref[...]))


def _conv_consts(w_dw, b_dw, ln_g, ln_b):
    return [w_dw, b_dw.reshape(1, BW), ln_g.reshape(1, BW), ln_b.reshape(1, BW)]


def _gmlp_chunk(z, g_ref, be_ref, ws_ref, bs_ref, gm_ref):
    z = z.astype(F32)
    z = 0.5 * z * (1.0 + lax.erf(z * (2.0 ** -0.5)))
    u, v = z[:, :BW], z[:, BW:]
    v = _layer_norm(v, g_ref[...], be_ref[...]).astype(BF16)
    full = _dot(ws_ref[...], v)
    sv = bs_ref[...]
    for g in range(GROUPS):
        sv = sv + full[g * CHUNK:(g + 1) * CHUNK, :] * gm_ref[g:g + 1, :]
    return u * sv


def _gmlp_consts(ln_g, ln_b, ws, bs):
    gm = np.repeat(np.eye(GROUPS, dtype=np.float32), GROUP_C, axis=1)
    return [ln_g.reshape(1, BW), ln_b.reshape(1, BW), ws.reshape(GROUPS * CHUNK, CHUNK).astype(BF16),
            jnp.repeat(bs.T, GROUP_C, axis=1), jnp.asarray(gm)]


@functools.lru_cache(maxsize=None)
def _dft_tables(l):
    n = np.arange(l, dtype=np.int64)
    ang = 2.0 * np.pi * ((n[:, None] * n[None, :]) % l) / l
    cs = np.concatenate([np.cos(ang), -np.sin(ang)], axis=1)
    c = np.arange(GROUP_C, dtype=np.int64)
    angc = 2.0 * np.pi * ((c[:, None] * c[None, :]) % GROUP_C) / GROUP_C
    eye = np.eye(GROUPS)
    return (cs.astype(np.float32), np.kron(eye, np.cos(angc)).astype(np.float32),
            np.kron(eye, np.sin(angc)).astype(np.float32))


def _fnet_kernel(l, scale, f_ref, wc_ref, ws_ref, cs_ref, out_ref, xx_ref):
    @pl.when(pl.program_id(1) == 0)
    def _():
        step = min(l, 512)
        for t in range(l // step):
            rows = slice(t * step, (t + 1) * step)
            x = f_ref[0, rows, :]
            xx_ref[t * step:(t + 1) * step, :] = _dot(x, wc_ref[...]).astype(BF16)
            xx_ref[l + t * step:l + (t + 1) * step, :] = _dot(x, ws_ref[...]).astype(BF16)

    out_ref[0] = (_dot(cs_ref[...], xx_ref[...]) * scale).astype(BF16)


def _fnet(f):
    b, l, _ = f.shape
    cs, wc, ws = _dft_tables(l)
    tr = l
    scale = float(1.0 / np.sqrt(l * GROUP_C))
    return pl.pallas_call(
        functools.partial(_fnet_kernel, l, scale),
        grid=(b, l // tr),
        in_specs=[pl.BlockSpec((1, l, BW), lambda i, j: (i, 0, 0)),
                  pl.BlockSpec((BW, BW), lambda i, j: (0, 0)),
                  pl.BlockSpec((BW, BW), lambda i, j: (0, 0)),
                  pl.BlockSpec((tr, 2 * l), lambda i, j: (j, 0), pipeline_mode=pl.Buffered(1))],
        out_specs=pl.BlockSpec((1, tr, BW), lambda i, j: (i, j, 0)),
        out_shape=jax.ShapeDtypeStruct((b, l, BW), BF16),
        scratch_shapes=[pltpu.VMEM((2 * l, BW), BF16)],
        compiler_params=_cparams("parallel", "arbitrary"),
        name="fnet",
    )(f, jnp.asarray(wc).astype(BF16), jnp.asarray(ws).astype(BF16), jnp.asarray(cs).astype(BF16))


def _merge_kernel(x_ref, gs_ref, sh_ref, ga_ref, p0_ref, p1_ref, p2_ref, p3_ref,
                  wg_ref, bg_ref, wo_ref, wout_ref, out_ref, hx_ref, m_ref):
    x = x_ref[0]
    hx_ref[...] = _modnorm(x, gs_ref[0], sh_ref[0]).astype(BF16)
    for i, p_ref in enumerate((p0_ref, p1_ref, p2_ref, p3_ref)):
        cols = slice(i * D_MODEL, (i + 1) * D_MODEL)
        gate = _sigmoid(_dot(hx_ref[...], wg_ref[:, cols]) + bg_ref[:, cols])
        term = gate * _dot(p_ref[0], wo_ref[i])
        if i == 0:
            m_ref[...] = term
        else:
            m_ref[...] += term
    out_ref[0] = x + ga_ref[0] * _dot(m_ref[...].astype(BF16), wout_ref[...])


def _merge(x, gs, sh, ga, pres, w_gate, b_gate, w_bo, w_out):
    b, l, d = x.shape
    tm = min(l, 512)
    vec = pl.BlockSpec((1, 1, d), lambda i, j: (i, 0, 0))
    pre = pl.BlockSpec((1, tm, BW), lambda i, j: (i, j, 0))
    return pl.pallas_call(
        _merge_kernel,
        grid=(b, l // tm),
        in_specs=[pl.BlockSpec((1, tm, d), lambda i, j: (i, j, 0)), vec, vec, vec, pre, pre, pre, pre,
                  pl.BlockSpec((d, 4 * d), lambda i, j: (0, 0)),
                  pl.BlockSpec((1, 4 * d), lambda i, j: (0, 0)),
                  pl.BlockSpec((4, BW, d), lambda i, j: (0, 0, 0)),
                  pl.BlockSpec((d, d), lambda i, j: (0, 0))],
        out_specs=pl.BlockSpec((1, tm, d), lambda i, j: (i, j, 0)),
        out_shape=jax.ShapeDtypeStruct((b, l, d), F32),
        scratch_shapes=[pltpu.VMEM((tm, d), BF16), pltpu.VMEM((tm, d), F32)],
        compiler_params=_cparams("parallel", "parallel"),
        name="merge",
    )(x, gs, sh, ga, *pres, w_gate, b_gate.reshape(1, 4 * d), w_bo, w_out)


FFN_TC = 256
FFN_ROWS = 256
FFN_SUB = 128


def _ffn_pad(grid_w, row_conv):
    return (grid_w if row_conv else 0) + 8


def _ffn_kernel(l, grid_w, row_conv, final_norm, x_ref, gs_ref, sh_ref, ga_ref, wa_ref, wb_ref, dw_ref,
                db_ref, wd_ref, gf_ref, out_ref, hx_ref, ac_ref, al_ref, ar_ref, b_ref, h_ref):
    j = pl.program_id(1)
    pad = _ffn_pad(grid_w, row_conv)
    n_tiles = l // FFN_ROWS

    @pl.when(j == 0)
    def _():
        for t in range(n_tiles):
            rows = slice(t * FFN_ROWS, (t + 1) * FFN_ROWS)
            hx_ref[rows, :] = _modnorm(x_ref[0, rows, :], gs_ref[0], sh_ref[0]).astype(BF16)
        for ref in (ac_ref, al_ref, ar_ref):
            ref[0:pad, :] = jnp.zeros((pad, FFN_TC), F32)
            ref[pad + l:2 * pad + l, :] = jnp.zeros((pad, FFN_TC), F32)
        out_ref[0] = jnp.zeros((l, D_MODEL), F32)

    col = lax.broadcasted_iota(jnp.int32, (FFN_ROWS, FFN_TC), 0) % grid_w
    not_first = col != 0
    not_last = col != grid_w - 1

    step = min(l, 512)
    for t in range(l // step):
        rows = slice(t * step, (t + 1) * step)
        ac_ref[pad + t * step:pad + (t + 1) * step, :] = _dot(hx_ref[rows, :], wa_ref[...])
        b_ref[rows, :] = _dot(hx_ref[rows, :], wb_ref[...])

    for t in range(n_tiles):
        lo = pad + t * FFN_ROWS
        al_ref[lo:lo + FFN_ROWS, :] = jnp.where(not_first, ac_ref[lo - 1:lo - 1 + FFN_ROWS, :], 0.0)
        ar_ref[lo:lo + FFN_ROWS, :] = jnp.where(not_last, ac_ref[lo + 1:lo + 1 + FFN_ROWS, :], 0.0)

    for t in range(l // FFN_SUB):
        acc = jnp.zeros((FFN_SUB, FFN_TC), F32) + db_ref[...]
        for dr in ((-1, 0, 1) if row_conv else (0,)):
            off = pad + t * FFN_SUB + dr * grid_w
            tap = 3 * (dr + 1)
            acc = acc + al_ref[off:off + FFN_SUB, :] * dw_ref[tap:tap + 1, :]
            acc = acc + ac_ref[off:off + FFN_SUB, :] * dw_ref[tap + 1:tap + 2, :]
            acc = acc + ar_ref[off:off + FFN_SUB, :] * dw_ref[tap + 2:tap + 3, :]
        rows = slice(t * FFN_SUB, (t + 1) * FFN_SUB)
        h_ref[rows, :] = (_silu(acc) * b_ref[rows, :]).astype(BF16)

    for t in range(l // step):
        rows = slice(t * step, (t + 1) * step)
        out_ref[0, rows, :] += _dot(h_ref[rows, :], wd_ref[...])

    @pl.when(j == pl.num_programs(1) - 1)
    def _():
        for t in range(l // FFN_ROWS):
            rows = slice(t * FFN_ROWS, (t + 1) * FFN_ROWS)
            y = x_ref[0, rows, :] + ga_ref[0] * out_ref[0, rows, :]
            if final_norm:
                y = y * lax.rsqrt(jnp.mean(y * y, axis=-1, keepdims=True) + EPS) * gf_ref[...]
            out_ref[0, rows, :] = y


def _conv_ffn(x, gs, sh, ga, w_up, dw, db, w_down, g_final, grid_w, row_conv, final_norm):
    b, l, d = x.shape
    nj = D_FF // FFN_TC
    pad = _ffn_pad(grid_w, row_conv)
    vec = pl.BlockSpec((1, 1, d), lambda i, j: (i, 0, 0))
    return pl.pallas_call(
        functools.partial(_ffn_kernel, l, grid_w, row_conv, final_norm),
        grid=(b, nj),
        in_specs=[pl.BlockSpec((1, l, d), lambda i, j: (i, 0, 0)), vec, vec, vec,
                  pl.BlockSpec((d, FFN_TC), lambda i, j: (0, j)),
                  pl.BlockSpec((d, FFN_TC), lambda i, j: (0, j + nj)),
                  pl.BlockSpec((9, FFN_TC), lambda i, j: (0, j)),
                  pl.BlockSpec((1, FFN_TC), lambda i, j: (0, j)),
                  pl.BlockSpec((FFN_TC, d), lambda i, j: (j, 0)),
                  pl.BlockSpec((1, d), lambda i, j: (0, 0))],
        out_specs=pl.BlockSpec((1, l, d), lambda i, j: (i, 0, 0)),
        out_shape=jax.ShapeDtypeStruct((b, l, d), F32),
        scratch_shapes=([pltpu.VMEM((l, d), BF16)] + [pltpu.VMEM((l + 2 * pad, FFN_TC), F32)] * 3
                        + [pltpu.VMEM((l, FFN_TC), F32), pltpu.VMEM((l, FFN_TC), BF16)]),
        compiler_params=_cparams("parallel", "arbitrary"),
        name="conv_ffn",
    )(x, gs, sh, ga, w_up, w_up, dw.reshape(9, D_FF), db.reshape(1, D_FF), w_down, g_final.reshape(1, d))


def _rope_tables(n_pos):
    half = RET_HEAD_DIM // 2
    inv_freq = ROPE_BASE ** (-jnp.arange(half, dtype=F32) / half)
    ang = jnp.arange(n_pos, dtype=jnp.int32).astype(F32)[:, None] * inv_freq[None, :]
    return jnp.tile(jnp.cos(ang), (1, RET_HEADS)), jnp.tile(jnp.sin(ang), (1, RET_HEADS))


def _split_half_perm():
    half = RET_HEAD_DIM // 2
    t, h, i = np.meshgrid(np.arange(2), np.arange(RET_HEADS), np.arange(half), indexing="ij")
    return (h * RET_HEAD_DIM + t * half + i).reshape(-1)


def kernel(x, c, ctx, c_ctx, w_ada, b_ada, g_norm1, g_norm2, w_in, b_gate, ret_decay, ret_gn, w_ret_o, conv_dw, conv_db, conv_ln_g, conv_ln_b, w_conv_o, gmlp_ln_g, gmlp_ln_b, gmlp_ws, gmlp_bs, w_gmlp_o, w_fnet_o, w_out, w_ffn_up, ffn_dw, ffn_db, w_ffn_down, g_final):
    bsz, seq, d = x.shape
    n_ctx = ctx.shape[1]
    cc = jnp.concatenate([c, c_ctx[None, :], jnp.zeros((7, d), F32)], axis=0)
    mods = _ada(cc, w_ada, b_ada)
    cos_t, sin_t = _rope_tables(n_ctx + seq)
    perm = _split_half_perm()
    zero_state = jnp.zeros((bsz, BW, BW), F32)
    xc = ctx

    def mixers(stream, l_idx, gs, sh, ga, cos_p, sin_p, sf0, sb0, tabs, want_out):
        w = w_in[l_idx]
        w_small = jnp.concatenate([w[:, :BW][:, perm], w[:, BW:2 * BW][:, perm], w[:, 2 * BW:SMALL_COLS]],
                                  axis=1).astype(BF16)
        qkvg, conf_h, gz, ff = _inproj(stream, gs, sh, w_small, cos_p, sin_p)
        ret_pre, conf_pre, gm_pre, sf, sb = _mixers(
            qkvg, conf_h, gz, sf0, sb0, tabs, ret_gn[l_idx].reshape(1, BW),
            _conv_consts(conv_dw[l_idx], conv_db[l_idx], conv_ln_g[l_idx], conv_ln_b[l_idx]),
            _gmlp_consts(gmlp_ln_g[l_idx], gmlp_ln_b[l_idx], gmlp_ws[l_idx], gmlp_bs[l_idx]))
        if not want_out:
            return None, sf, sb
        fn_pre = _fnet(ff)
        w_bo = jnp.stack([w_ret_o[l_idx], w_conv_o[l_idx], w_gmlp_o[l_idx], w_fnet_o[l_idx]]).astype(BF16)
        new = _merge(stream, gs, sh, ga, (ret_pre, conf_pre, gm_pre, fn_pre),
                     w[:, GATE_OFF:].astype(BF16), b_gate[l_idx], w_bo, w_out[l_idx].astype(BF16))
        return new, sf, sb

    for l_idx in range(DEPTH):
        last = l_idx == DEPTH - 1
        lat = [m[:, None, :] for m in jnp.split(mods[l_idx, :bsz], 6, axis=-1)]
        cm = [jnp.broadcast_to(m[None, None, :], (bsz, 1, d)) for m in jnp.split(mods[l_idx, bsz], 6, axis=-1)]
        g1 = g_norm1[l_idx][None, None, :]
        g2 = g_norm2[l_idx][None, None, :]
        tabs = _retention_tables(jax.nn.log_sigmoid(ret_decay[l_idx].astype(F32)))
        w_up = w_ffn_up[l_idx].astype(BF16)
        w_down = w_ffn_down[l_idx].astype(BF16)

        xc_mix, s_f, s_b = mixers(xc, l_idx, g1 * (1 + cm[1]), cm[0], cm[2], cos_t[:n_ctx], sin_t[:n_ctx],
                                  zero_state, zero_state, tabs, not last)
        x, _, _ = mixers(x, l_idx, g1 * (1 + lat[1]), lat[0], lat[2], cos_t[n_ctx:], sin_t[n_ctx:],
                         s_f, s_b, tabs, True)
        x = _conv_ffn(x, g2 * (1 + lat[4]), lat[3], lat[5], w_up, ffn_dw[l_idx], ffn_db[l_idx], w_down,
                      g_final, GRID_W, True, last)
        if not last:
            fold = max(1, seq // n_ctx)
            while bsz % fold:
                fold //= 2
            xc = _conv_ffn(xc_mix.reshape(bsz // fold, fold * n_ctx, d), (g2 * (1 + cm[4]))[:bsz // fold],
                           cm[3][:bsz // fold], cm[5][:bsz // fold], w_up, ffn_dw[l_idx], ffn_db[l_idx],
                           w_down, g_final, n_ctx, False, False).reshape(bsz, n_ctx, d)
    return x
```

```python
import functools

import jax
import jax.numpy as jnp
import numpy as np
from jax import lax
from jax.experimental import pallas as pl
from jax.experimental.pallas import tpu as pltpu

D_MODEL = 1024
DEPTH = 2
GRID_W = 64
RET_HEADS = 4
RET_HEAD_DIM = 64
BW = 256
CHUNK = 128
ROPE_BASE = 10000.0
CONV_KERNEL = 31
GROUPS = 4
GROUP_C = BW // GROUPS
D_FF = ((8 * D_MODEL // 3 + 127) // 128) * 128
EPS = 1e-6
SMALL_COLS = 9 * BW
GATE_OFF = SMALL_COLS

F32 = jnp.float32
BF16 = jnp.bfloat16
HIGHEST = lax.Precision.HIGHEST
VMEM_LIMIT_BYTES = 56 * 1024 * 1024


def _cparams(*sem):
    return pltpu.CompilerParams(dimension_semantics=sem, vmem_limit_bytes=VMEM_LIMIT_BYTES)


def _dot(a, b):
    return jnp.dot(a, b, preferred_element_type=F32)


def _modnorm(x, gs, sh):
    ms = jnp.mean(x * x, axis=-1, keepdims=True)
    return x * lax.rsqrt(ms + EPS) * gs + sh


def _sigmoid(x):
    return 1.0 / (1.0 + jnp.exp(-x))


def _silu(x):
    return x * _sigmoid(x)


def _layer_norm(x, g, b):
    xc = x - jnp.mean(x, axis=-1, keepdims=True)
    return xc * lax.rsqrt(jnp.mean(xc * xc, axis=-1, keepdims=True) + EPS) * g + b


def _ada_kernel(c_ref, w_ref, b_ref, o_ref):
    o_ref[0] = jnp.dot(_silu(c_ref[...]), w_ref[0], precision=HIGHEST,
                       preferred_element_type=F32) + b_ref[0]


def _ada(cc, w_ada, b_ada):
    n, d = cc.shape
    cols = w_ada.shape[-1]
    tn = 1536
    return pl.pallas_call(
        _ada_kernel,
        grid=(DEPTH, cols // tn),
        in_specs=[pl.BlockSpec((n, d), lambda l, j: (0, 0)),
                  pl.BlockSpec((1, d, tn), lambda l, j: (l, 0, j)),
                  pl.BlockSpec((1, 1, tn), lambda l, j: (l, 0, j))],
        out_specs=pl.BlockSpec((1, n, tn), lambda l, j: (l, 0, j)),
        out_shape=jax.ShapeDtypeStruct((DEPTH, n, cols), F32),
        compiler_params=_cparams("parallel", "parallel"),
        name="ada",
    )(cc, w_ada, b_ada.reshape(DEPTH, 1, cols))


def _inproj_kernel(x_ref, gs_ref, sh_ref, w_ref, cos_ref, sin_ref,
                   qkvg_ref, conf_ref, gz_ref, ff_ref, hx_ref):
    hx_ref[...] = _modnorm(x_ref[0], gs_ref[0], sh_ref[0]).astype(BF16)
    cos = cos_ref[...]
    sin = sin_ref[...]
    half = BW // 2

    def rot(p, scale):
        p1, p2 = p[:, :half], p[:, half:]
        return jnp.concatenate([(p1 * cos - p2 * sin) * scale, (p1 * sin + p2 * cos) * scale], axis=-1)

    pq = _dot(hx_ref[...], w_ref[:, 0:BW])
    qkvg_ref[0, :, 0:BW] = rot(pq, 1.0).astype(BF16)
    pk = _dot(hx_ref[...], w_ref[:, BW:2 * BW])
    qkvg_ref[0, :, BW:2 * BW] = rot(pk, RET_HEAD_DIM ** -0.5).astype(BF16)
    pv = _dot(hx_ref[...], w_ref[:, 2 * BW:4 * BW])
    qkvg_ref[0, :, 2 * BW:4 * BW] = pv.astype(BF16)
    pc = _dot(hx_ref[...], w_ref[:, 4 * BW:6 * BW])
    conf_ref[0] = (pc[:, :BW] * _sigmoid(pc[:, BW:])).astype(BF16)
    gz_ref[0] = _dot(hx_ref[...], w_ref[:, 6 * BW:8 * BW]).astype(BF16)
    ff_ref[0] = _dot(hx_ref[...], w_ref[:, 8 * BW:9 * BW]).astype(BF16)


def _inproj(x, gs, sh, w_small, cos_t, sin_t):
    b, l, d = x.shape
    tm = min(l, 512)
    vec = pl.BlockSpec((1, 1, d), lambda i, j: (i, 0, 0))
    tab = pl.BlockSpec((tm, BW // 2), lambda i, j: (j, 0))

    def out(c):
        return pl.BlockSpec((1, tm, c), lambda i, j: (i, j, 0))

    return pl.pallas_call(
        _inproj_kernel,
        grid=(b, l // tm),
        in_specs=[pl.BlockSpec((1, tm, d), lambda i, j: (i, j, 0)), vec, vec,
                  pl.BlockSpec((d, SMALL_COLS), lambda i, j: (0, 0)), tab, tab],
        out_specs=[out(4 * BW), out(BW), out(2 * BW), out(BW)],
        out_shape=[jax.ShapeDtypeStruct((b, l, c), BF16) for c in (4 * BW, BW, 2 * BW, BW)],
        scratch_shapes=[pltpu.VMEM((tm, d), BF16)],
        compiler_params=_cparams("parallel", "parallel"),
        name="inproj",
    )(x, gs, sh, w_small, cos_t, sin_t)


def _mixer_kernel(n_chunks, qkvg_ref, conf_ref, gz_ref, sf0_ref, sb0_ref,
                  mst_ref, wqf_ref, wqb_ref, wkf_ref, wkb_ref, cdf_ref, cdb_ref, bd_ref, avg_ref, gn_ref, hm_ref,
                  cw_ref, cb_ref, cg_ref, cbe_ref, mg_ref, mbe_ref, mws_ref, mbs_ref, mgm_ref,
                  out_ref, cout_ref, mout_ref, sf_ref, sb_ref, o_scr, uf_scr, ub_scr, st_scr, hp_scr):
    unroll = 4 if n_chunks % 4 == 0 else (2 if n_chunks % 2 == 0 else 1)
    l = n_chunks * CHUNK
    hp_scr[0:CONV_PAD, :] = jnp.zeros((CONV_PAD, BW), F32)
    hp_scr[CONV_PAD + l:2 * CONV_PAD + l, :] = jnp.zeros((CONV_PAD, BW), F32)

    def chunk_rows(n):
        return pl.ds(pl.multiple_of(n * CHUNK, CHUNK), CHUNK)

    def state_update(k, wk_ref, v):
        kdec = (k * wk_ref[...]).astype(BF16)
        upd = lax.dot_general(kdec, v, (((0,), (0,)), ((), ())), preferred_element_type=F32)
        return bd_ref[...] * upd

    def intra(n, carry):
        rows = chunk_rows(n)
        q = qkvg_ref[0, rows, 0:BW]
        k = qkvg_ref[0, rows, BW:2 * BW]
        v = qkvg_ref[0, rows, 2 * BW:3 * BW]
        kf = k.astype(F32)
        qst = jnp.concatenate([q * hm_ref[h] for h in range(RET_HEADS)], axis=0)
        s = lax.dot_general(qst, k, (((1,), (1,)), ((), ())), preferred_element_type=F32)
        p = (s * mst_ref[...]).astype(BF16)
        pcat = jnp.concatenate([p[h * CHUNK:(h + 1) * CHUNK] for h in range(RET_HEADS)], axis=1)
        vbd = jnp.concatenate([v * hm_ref[RET_HEADS + h] for h in range(RET_HEADS)], axis=0)
        o_scr[rows, :] = _dot(pcat, vbd)
        uf_scr[n] = state_update(kf, wkf_ref, v)
        ub_scr[n] = state_update(kf, wkb_ref, v)
        mout_ref[0, rows, :] = _gmlp_chunk(gz_ref[0, rows, :], mg_ref, mbe_ref, mws_ref, mbs_ref,
                                           mgm_ref).astype(BF16)
        hp_scr[pl.ds(pl.multiple_of(n * CHUNK + CONV_PAD, 8), CHUNK), :] = conf_ref[0, rows, :].astype(F32)
        return carry

    lax.fori_loop(0, n_chunks, intra, 0, unroll=unroll)

    sf_ref[0] = sf0_ref[0]
    sb_ref[0] = sb0_ref[0]

    def scan(i, carry):
        st_scr[i, 0:BW, :] = sf_ref[0].astype(BF16)
        sf_ref[0] = cdf_ref[...] * sf_ref[0] + uf_scr[i]
        n = n_chunks - 1 - i
        st_scr[n, BW:2 * BW, :] = sb_ref[0].astype(BF16)
        sb_ref[0] = cdb_ref[...] * sb_ref[0] + ub_scr[n]
        return carry

    lax.fori_loop(0, n_chunks, scan, 0)

    def group_mean(t):
        hi = t.astype(BF16)
        lo = (t - hi.astype(F32)).astype(BF16)
        return _dot(jnp.concatenate([hi, lo], axis=1), avg_ref[...])

    def finish(n, carry):
        rows = chunk_rows(n)
        q = qkvg_ref[0, rows, 0:BW].astype(F32)
        qdec = jnp.concatenate([(q * wqf_ref[...]).astype(BF16), (q * wqb_ref[...]).astype(BF16)], axis=1)
        o = o_scr[rows, :] + _dot(qdec, st_scr[n])
        oc = o - group_mean(o)
        y = oc * lax.rsqrt(group_mean(oc * oc) + EPS) * gn_ref[...]
        g = qkvg_ref[0, rows, 3 * BW:4 * BW].astype(F32)
        out_ref[0, rows, :] = (_silu(g) * y).astype(BF16)
        for t in range(CHUNK // CONV_TILE):
            base = pl.multiple_of(n * CHUNK + t * CONV_TILE, CONV_TILE)
            cout_ref[0, pl.ds(base, CONV_TILE), :] = _conv_tile(hp_scr, base, cw_ref, cb_ref, cg_ref,
                                                                cbe_ref).astype(BF16)
        return carry

    lax.fori_loop(0, n_chunks, finish, 0, unroll=unroll)


def _mixers(qkvg, conf_h, gz, sf0, sb0, tabs, gn, conv_consts, gmlp_consts):
    b, l, _ = qkvg.shape
    n_chunks = l // CHUNK
    st = pl.BlockSpec((1, BW, BW), lambda i: (i, 0, 0))

    def seq(c):
        return pl.BlockSpec((1, l, c), lambda i: (i, 0, 0))

    def full(a):
        return pl.BlockSpec(a.shape, lambda i: (0,) * a.ndim)

    consts = ([tabs[k] for k in ("mst", "wqf", "wqb", "wkf", "wkb", "cdf", "cdb", "bd", "avg")] + [gn, tabs["hm"]]
              + conv_consts + gmlp_consts)
    return pl.pallas_call(
        functools.partial(_mixer_kernel, n_chunks),
        grid=(b,),
        in_specs=[seq(4 * BW), seq(BW), seq(2 * BW), st, st] + [full(a) for a in consts],
        out_specs=[seq(BW), seq(BW), seq(BW), st, st],
        out_shape=[jax.ShapeDtypeStruct((b, l, BW), BF16)] * 3 + [jax.ShapeDtypeStruct((b, BW, BW), F32)] * 2,
        scratch_shapes=[pltpu.VMEM((l, BW), F32),
                        pltpu.VMEM((n_chunks, BW, BW), F32),
                        pltpu.VMEM((n_chunks, BW, BW), F32),
                        pltpu.VMEM((n_chunks, 2 * BW, BW), BF16),
                        pltpu.VMEM((l + 2 * CONV_PAD, BW), F32)],
        compiler_params=_cparams("parallel"),
        name="mixers",
    )(qkvg, conf_h, gz, sf0, sb0, *consts)


def _state_kernel(n_chunks, k_ref, v_ref, wkf_ref, wkb_ref, cdf_ref, cdb_ref, bd_ref, sf_ref, sb_ref):
    sf_ref[0] = jnp.zeros((BW, BW), F32)
    sb_ref[0] = jnp.zeros((BW, BW), F32)

    def increment(n, wk_ref):
        rows = pl.ds(pl.multiple_of(n * CHUNK, CHUNK), CHUNK)
        kdec = (k_ref[0, rows, :].astype(F32) * wk_ref[...]).astype(BF16)
        upd = lax.dot_general(kdec, v_ref[0, rows, :], (((0,), (0,)), ((), ())), preferred_element_type=F32)
        return bd_ref[...] * upd

    def scan(i, carry):
        sf_ref[0] = cdf_ref[...] * sf_ref[0] + increment(i, wkf_ref)
        sb_ref[0] = cdb_ref[...] * sb_ref[0] + increment(n_chunks - 1 - i, wkb_ref)
        return carry

    lax.fori_loop(0, n_chunks, scan, 0)


def _final_states(qkvg, tabs):
    b, l, _ = qkvg.shape
    st = pl.BlockSpec((1, BW, BW), lambda i: (i, 0, 0))

    def full(a):
        return pl.BlockSpec(a.shape, lambda i: (0,) * a.ndim)

    consts = [tabs[k] for k in ("wkf", "wkb", "cdf", "cdb", "bd")]
    return pl.pallas_call(
        functools.partial(_state_kernel, l // CHUNK),
        grid=(b,),
        in_specs=[pl.BlockSpec((1, l, BW), lambda i: (i, 0, 1)),
                  pl.BlockSpec((1, l, BW), lambda i: (i, 0, 2))] + [full(a) for a in consts],
        out_specs=[st, st],
        out_shape=[jax.ShapeDtypeStruct((b, BW, BW), F32)] * 2,
        compiler_params=_cparams("parallel"),
        name="final_states",
    )(qkvg, qkvg, *consts)


def _retention_tables(log_g):
    lf, lb = log_g[0], log_g[1]
    idx = jnp.arange(CHUNK, dtype=F32)
    diff = idx[:, None] - idx[None, :]
    m = (jnp.where(diff > 0, jnp.exp(lf[:, None, None] * jnp.maximum(diff, 0.0)), 0.0)
         + jnp.where(diff < 0, jnp.exp(lb[:, None, None] * jnp.maximum(-diff, 0.0)), 0.0)
         + jnp.where(diff == 0, 2.0, 0.0))
    lane = np.arange(BW)
    hq = (lane % (BW // 2)) // (RET_HEAD_DIM // 2)
    hv = lane // RET_HEAD_DIM
    lfq, lbq = lf[hq][None, :], lb[hq][None, :]
    i = idx[:, None]
    hm = np.concatenate([(hq[None, :] == np.arange(RET_HEADS)[:, None]),
                         (hv[None, :] == np.arange(RET_HEADS)[:, None])], axis=0).astype(np.float32)
    return dict(
        mst=m.reshape(RET_HEADS * CHUNK, CHUNK),
        wqf=jnp.exp(lfq * (i + 1.0)), wqb=jnp.exp(lbq * (CHUNK - i)),
        wkf=jnp.exp(lfq * (CHUNK - 1.0 - i)), wkb=jnp.exp(lbq * i),
        cdf=jnp.broadcast_to(jnp.exp(lf[hq] * CHUNK)[:, None], (BW, BW)),
        cdb=jnp.broadcast_to(jnp.exp(lb[hq] * CHUNK)[:, None], (BW, BW)),
        bd=jnp.asarray((hq[:, None] == hv[None, :]).astype(np.float32)),
        avg=jnp.asarray(np.tile((hv[:, None] == hv[None, :]).astype(np.float32) / RET_HEAD_DIM,
                                (2, 1))).astype(BF16),
        hm=jnp.asarray(np.broadcast_to(hm[:, None, :], (2 * RET_HEADS, CHUNK, BW))).astype(BF16),
    )


CONV_TILE = 128
CONV_PAD = 16


def _conv_tile(hp_ref, base, w_ref, b_ref, g_ref, be_ref):
    first = CONV_PAD - (CONV_KERNEL - 1) // 2
    win = hp_ref[pl.ds(base, CONV_TILE + 2 * CONV_PAD), :]
    acc = jnp.zeros((CONV_TILE, BW), F32) + b_ref[...]
    for s in range(8):
        part = None
        for m in range(4):
            k = 8 * m + s - first
            if 0 <= k < CONV_KERNEL:
                term = win[8 * m:8 * m + CONV_TILE + 8, :] * w_ref[k:k + 1, :]
                part = term if part is None else part + term
        if part is not None:
            acc = acc + part[s:s + CONV_TILE, :]
    return _silu(_layer_norm(acc, g_ref[...], be_ref[...]))


def _conv_consts(w_dw, b_dw, ln_g, ln_b):
    return [w_dw, b_dw.reshape(1, BW), ln_g.reshape(1, BW), ln_b.reshape(1, BW)]


def _gmlp_chunk(z, g_ref, be_ref, ws_ref, bs_ref, gm_ref):
    z = z.astype(F32)
    z = 0.5 * z * (1.0 + lax.erf(z * (2.0 ** -0.5)))
    u, v = z[:, :BW], z[:, BW:]
    v = _layer_norm(v, g_ref[...], be_ref[...]).astype(BF16)
    full = _dot(ws_ref[...], v)
    sv = bs_ref[...]
    for g in range(GROUPS):
        sv = sv + full[g * CHUNK:(g + 1) * CHUNK, :] * gm_ref[g:g + 1, :]
    return u * sv


def _gmlp_consts(ln_g, ln_b, ws, bs):
    gm = np.repeat(np.eye(GROUPS, dtype=np.float32), GROUP_C, axis=1)
    return [ln_g.reshape(1, BW), ln_b.reshape(1, BW), ws.reshape(GROUPS * CHUNK, CHUNK).astype(BF16),
            jnp.repeat(bs.T, GROUP_C, axis=1), jnp.asarray(gm)]


@functools.lru_cache(maxsize=None)
def _dft_tables(l):
    n = np.arange(l, dtype=np.int64)
    ang = 2.0 * np.pi * ((n[:, None] * n[None, :]) % l) / l
    cs = np.concatenate([np.cos(ang), -np.sin(ang)], axis=1)
    c = np.arange(GROUP_C, dtype=np.int64)
    angc = 2.0 * np.pi * ((c[:, None] * c[None, :]) % GROUP_C) / GROUP_C
    eye = np.eye(GROUPS)
    return (cs.astype(np.float32), np.kron(eye, np.cos(angc)).astype(np.float32),
            np.kron(eye, np.sin(angc)).astype(np.float32))


def _fnet_kernel(l, scale, f_ref, wc_ref, ws_ref, cs_ref, out_ref, xx_ref):
    @pl.when(pl.program_id(1) == 0)
    def _():
        step = min(l, 512)
        for t in range(l // step):
            rows = slice(t * step, (t + 1) * step)
            x = f_ref[0, rows, :]
            xx_ref[t * step:(t + 1) * step, :] = _dot(x, wc_ref[...]).astype(BF16)
            xx_ref[l + t * step:l + (t + 1) * step, :] = _dot(x, ws_ref[...]).astype(BF16)

    out_ref[0] = (_dot(cs_ref[...], xx_ref[...]) * scale).astype(BF16)


def _fnet(f):
    b, l, _ = f.shape
    cs, wc, ws = _dft_tables(l)
    tr = l
    scale = float(1.0 / np.sqrt(l * GROUP_C))
    return pl.pallas_call(
        functools.partial(_fnet_kernel, l, scale),
        grid=(b, l // tr),
        in_specs=[pl.BlockSpec((1, l, BW), lambda i, j: (i, 0, 0)),
                  pl.BlockSpec((BW, BW), lambda i, j: (0, 0)),
                  pl.BlockSpec((BW, BW), lambda i, j: (0, 0)),
                  pl.BlockSpec((tr, 2 * l), lambda i, j: (j, 0), pipeline_mode=pl.Buffered(1))],
        out_specs=pl.BlockSpec((1, tr, BW), lambda i, j: (i, j, 0)),
        out_shape=jax.ShapeDtypeStruct((b, l, BW), BF16),
        scratch_shapes=[pltpu.VMEM((2 * l, BW), BF16)],
        compiler_params=_cparams("parallel", "arbitrary"),
        name="fnet",
    )(f, jnp.asarray(wc).astype(BF16), jnp.asarray(ws).astype(BF16), jnp.asarray(cs).astype(BF16))


def _merge_kernel(x_ref, gs_ref, sh_ref, ga_ref, p0_ref, p1_ref, p2_ref, p3_ref,
                  wg_ref, bg_ref, wo_ref, wout_ref, out_ref, hx_ref, m_ref):
    x = x_ref[0]
    hx_ref[...] = _modnorm(x, gs_ref[0], sh_ref[0]).astype(BF16)
    for i, p_ref in enumerate((p0_ref, p1_ref, p2_ref, p3_ref)):
        cols = slice(i * D_MODEL, (i + 1) * D_MODEL)
        gate = _sigmoid(_dot(hx_ref[...], wg_ref[:, cols]) + bg_ref[:, cols])
        term = gate * _dot(p_ref[0], wo_ref[i])
        if i == 0:
            m_ref[...] = term
        else:
            m_ref[...] += term
    out_ref[0] = x + ga_ref[0] * _dot(m_ref[...].astype(BF16), wout_ref[...])


def _merge(x, gs, sh, ga, pres, w_gate, b_gate, w_bo, w_out):
    b, l, d = x.shape
    tm = min(l, 512)
    vec = pl.BlockSpec((1, 1, d), lambda i, j: (i, 0, 0))
    pre = pl.BlockSpec((1, tm, BW), lambda i, j: (i, j, 0))
    return pl.pallas_call(
        _merge_kernel,
        grid=(b, l // tm),
        in_specs=[pl.BlockSpec((1, tm, d), lambda i, j: (i, j, 0)), vec, vec, vec, pre, pre, pre, pre,
                  pl.BlockSpec((d, 4 * d), lambda i, j: (0, 0)),
                  pl.BlockSpec((1, 4 * d), lambda i, j: (0, 0)),
                  pl.BlockSpec((4, BW, d), lambda i, j: (0, 0, 0)),
                  pl.BlockSpec((d, d), lambda i, j: (0, 0))],
        out_specs=pl.BlockSpec((1, tm, d), lambda i, j: (i, j, 0)),
        out_shape=jax.ShapeDtypeStruct((b, l, d), F32),
        scratch_shapes=[pltpu.VMEM((tm, d), BF16), pltpu.VMEM((tm, d), F32)],
        compiler_params=_cparams("parallel", "parallel"),
        name="merge",
    )(x, gs, sh, ga, *pres, w_gate, b_gate.reshape(1, 4 * d), w_bo, w_out)


FFN_TC = 256
FFN_ROWS = 256
FFN_SUB = 128


def _ffn_pad(grid_w, row_conv):
    return (grid_w if row_conv else 0) + 8


def _ffn_kernel(l, grid_w, row_conv, final_norm, x_ref, gs_ref, sh_ref, ga_ref, wa_ref, wb_ref, dw_ref,
                db_ref, wd_ref, gf_ref, out_ref, hx_ref, ac_ref, al_ref, ar_ref, b_ref, h_ref, wds_ref):
    j = pl.program_id(1)
    pad = _ffn_pad(grid_w, row_conv)
    n_tiles = l // FFN_ROWS

    @pl.when(j == 0)
    def _():
        for t in range(n_tiles):
            rows = slice(t * FFN_ROWS, (t + 1) * FFN_ROWS)
            x = x_ref[0, rows, :]
            hx_ref[rows, :] = _modnorm(x, gs_ref[0], sh_ref[0]).astype(BF16)
            out_ref[0, rows, :] = x
        for ref in (ac_ref, al_ref, ar_ref):
            ref[0:pad, :] = jnp.zeros((pad, FFN_TC), F32)
            ref[pad + l:2 * pad + l, :] = jnp.zeros((pad, FFN_TC), F32)

    wds_ref[...] = (wd_ref[...].astype(F32) * ga_ref[0]).astype(BF16)

    col = lax.broadcasted_iota(jnp.int32, (FFN_ROWS, FFN_TC), 0) % grid_w
    not_first = col != 0
    not_last = col != grid_w - 1

    step = min(l, 512)
    for t in range(l // step):
        rows = slice(t * step, (t + 1) * step)
        ac_ref[pad + t * step:pad + (t + 1) * step, :] = _dot(hx_ref[rows, :], wa_ref[...])
        b_ref[rows, :] = _dot(hx_ref[rows, :], wb_ref[...])

    for t in range(n_tiles):
        lo = pad + t * FFN_ROWS
        al_ref[lo:lo + FFN_ROWS, :] = jnp.where(not_first, ac_ref[lo - 1:lo - 1 + FFN_ROWS, :], 0.0)
        ar_ref[lo:lo + FFN_ROWS, :] = jnp.where(not_last, ac_ref[lo + 1:lo + 1 + FFN_ROWS, :], 0.0)

    for t in range(l // FFN_SUB):
        acc = jnp.zeros((FFN_SUB, FFN_TC), F32) + db_ref[...]
        for dr in ((-1, 0, 1) if row_conv else (0,)):
            off = pad + t * FFN_SUB + dr * grid_w
            tap = 3 * (dr + 1)
            acc = acc + al_ref[off:off + FFN_SUB, :] * dw_ref[tap:tap + 1, :]
            acc = acc + ac_ref[off:off + FFN_SUB, :] * dw_ref[tap + 1:tap + 2, :]
            acc = acc + ar_ref[off:off + FFN_SUB, :] * dw_ref[tap + 2:tap + 3, :]
        rows = slice(t * FFN_SUB, (t + 1) * FFN_SUB)
        h_ref[rows, :] = (_silu(acc) * b_ref[rows, :]).astype(BF16)

    for t in range(l // step):
        rows = slice(t * step, (t + 1) * step)
        out_ref[0, rows, :] += _dot(h_ref[rows, :], wds_ref[...])

    if final_norm:
        @pl.when(j == pl.num_programs(1) - 1)
        def _():
            for t in range(l // FFN_ROWS):
                rows = slice(t * FFN_ROWS, (t + 1) * FFN_ROWS)
                y = out_ref[0, rows, :]
                out_ref[0, rows, :] = y * lax.rsqrt(jnp.mean(y * y, axis=-1, keepdims=True) + EPS) * gf_ref[...]


def _conv_ffn(x, gs, sh, ga, w_up, dw, db, w_down, g_final, grid_w, row_conv, final_norm):
    b, l, d = x.shape
    nj = D_FF // FFN_TC
    pad = _ffn_pad(grid_w, row_conv)
    vec = pl.BlockSpec((1, 1, d), lambda i, j: (i, 0, 0))
    return pl.pallas_call(
        functools.partial(_ffn_kernel, l, grid_w, row_conv, final_norm),
        grid=(b, nj),
        in_specs=[pl.BlockSpec((1, l, d), lambda i, j: (i, 0, 0)), vec, vec, vec,
                  pl.BlockSpec((d, FFN_TC), lambda i, j: (0, j)),
                  pl.BlockSpec((d, FFN_TC), lambda i, j: (0, j + nj)),
                  pl.BlockSpec((9, FFN_TC), lambda i, j: (0, j)),
                  pl.BlockSpec((1, FFN_TC), lambda i, j: (0, j)),
                  pl.BlockSpec((FFN_TC, d), lambda i, j: (j, 0)),
                  pl.BlockSpec((1, d), lambda i, j: (0, 0))],
        out_specs=pl.BlockSpec((1, l, d), lambda i, j: (i, 0, 0)),
        out_shape=jax.ShapeDtypeStruct((b, l, d), F32),
        scratch_shapes=([pltpu.VMEM((l, d), BF16)] + [pltpu.VMEM((l + 2 * pad, FFN_TC), F32)] * 3
                        + [pltpu.VMEM((l, FFN_TC), F32), pltpu.VMEM((l, FFN_TC), BF16),
                           pltpu.VMEM((FFN_TC, d), BF16)]),
        compiler_params=_cparams("parallel", "arbitrary"),
        name="conv_ffn",
    )(x, gs, sh, ga, w_up, w_up, dw.reshape(9, D_FF), db.reshape(1, D_FF), w_down, g_final.reshape(1, d))


def _rope_tables(n_pos):
    half = RET_HEAD_DIM // 2
    inv_freq = ROPE_BASE ** (-jnp.arange(half, dtype=F32) / half)
    ang = jnp.arange(n_pos, dtype=jnp.int32).astype(F32)[:, None] * inv_freq[None, :]
    return jnp.tile(jnp.cos(ang), (1, RET_HEADS)), jnp.tile(jnp.sin(ang), (1, RET_HEADS))


def _split_half_perm():
    half = RET_HEAD_DIM // 2
    t, h, i = np.meshgrid(np.arange(2), np.arange(RET_HEADS), np.arange(half), indexing="ij")
    return (h * RET_HEAD_DIM + t * half + i).reshape(-1)


def kernel(x, c, ctx, c_ctx, w_ada, b_ada, g_norm1, g_norm2, w_in, b_gate, ret_decay, ret_gn, w_ret_o, conv_dw, conv_db, conv_ln_g, conv_ln_b, w_conv_o, gmlp_ln_g, gmlp_ln_b, gmlp_ws, gmlp_bs, w_gmlp_o, w_fnet_o, w_out, w_ffn_up, ffn_dw, ffn_db, w_ffn_down, g_final):
    bsz, seq, d = x.shape
    n_ctx = ctx.shape[1]
    cc = jnp.concatenate([c, c_ctx[None, :], jnp.zeros((7, d), F32)], axis=0)
    mods = _ada(cc, w_ada, b_ada)
    cos_t, sin_t = _rope_tables(n_ctx + seq)
    perm = _split_half_perm()
    zero_state = jnp.zeros((bsz, BW, BW), F32)
    xc = ctx

    def mixers(stream, l_idx, gs, sh, ga, cos_p, sin_p, sf0, sb0, tabs, want_out):
        w = w_in[l_idx]
        w_small = jnp.concatenate([w[:, :BW][:, perm], w[:, BW:2 * BW][:, perm], w[:, 2 * BW:SMALL_COLS]],
                                  axis=1).astype(BF16)
        qkvg, conf_h, gz, ff = _inproj(stream, gs, sh, w_small, cos_p, sin_p)
        if not want_out:
            return (None,) + tuple(_final_states(qkvg, tabs))
        ret_pre, conf_pre, gm_pre, sf, sb = _mixers(
            qkvg, conf_h, gz, sf0, sb0, tabs, ret_gn[l_idx].reshape(1, BW),
            _conv_consts(conv_dw[l_idx], conv_db[l_idx], conv_ln_g[l_idx], conv_ln_b[l_idx]),
            _gmlp_consts(gmlp_ln_g[l_idx], gmlp_ln_b[l_idx], gmlp_ws[l_idx], gmlp_bs[l_idx]))
        fn_pre = _fnet(ff)
        w_bo = jnp.stack([w_ret_o[l_idx], w_conv_o[l_idx], w_gmlp_o[l_idx], w_fnet_o[l_idx]]).astype(BF16)
        new = _merge(stream, gs, sh, ga, (ret_pre, conf_pre, gm_pre, fn_pre),
                     w[:, GATE_OFF:].astype(BF16), b_gate[l_idx], w_bo, w_out[l_idx].astype(BF16))
        return new, sf, sb

    for l_idx in range(DEPTH):
        last = l_idx == DEPTH - 1
        lat = [m[:, None, :] for m in jnp.split(mods[l_idx, :bsz], 6, axis=-1)]
        cm = [jnp.broadcast_to(m[None, None, :], (bsz, 1, d)) for m in jnp.split(mods[l_idx, bsz], 6, axis=-1)]
        g1 = g_norm1[l_idx][None, None, :]
        g2 = g_norm2[l_idx][None, None, :]
        tabs = _retention_tables(jax.nn.log_sigmoid(ret_decay[l_idx].astype(F32)))
        w_up = w_ffn_up[l_idx].astype(BF16)
        w_down = w_ffn_down[l_idx].astype(BF16)

        xc_mix, s_f, s_b = mixers(xc, l_idx, g1 * (1 + cm[1]), cm[0], cm[2], cos_t[:n_ctx], sin_t[:n_ctx],
                                  zero_state, zero_state, tabs, not last)
        x, _, _ = mixers(x, l_idx, g1 * (1 + lat[1]), lat[0], lat[2], cos_t[n_ctx:], sin_t[n_ctx:],
                         s_f, s_b, tabs, True)
        x = _conv_ffn(x, g2 * (1 + lat[4]), lat[3], lat[5], w_up, ffn_dw[l_idx], ffn_db[l_idx], w_down,
                      g_final, GRID_W, True, last)
        if not last:
            fold = max(1, seq // n_ctx)
            while bsz % fold:
                fold //= 2
            xc = _conv_ffn(xc_mix.reshape(bsz // fold, fold * n_ctx, d), (g2 * (1 + cm[4]))[:bsz // fold],
                           cm[3][:bsz // fold], cm[5][:bsz // fold], w_up, ffn_dw[l_idx], ffn_db[l_idx],
                           w_down, g_final, n_ctx, False, False).reshape(bsz, n_ctx, d)
    return x
```

```python
import functools

import jax
import jax.numpy as jnp
import numpy as np
from jax import lax
from jax.experimental import pallas as pl
from jax.experimental.pallas import tpu as pltpu

D_MODEL = 1024
DEPTH = 2
GRID_W = 64
RET_HEADS = 4
RET_HEAD_DIM = 64
BW = 256
CHUNK = 128
ROPE_BASE = 10000.0
CONV_KERNEL = 31
GROUPS = 4
GROUP_C = BW // GROUPS
D_FF = ((8 * D_MODEL // 3 + 127) // 128) * 128
EPS = 1e-6
SMALL_COLS = 9 * BW
GATE_OFF = SMALL_COLS

F32 = jnp.float32
BF16 = jnp.bfloat16
HIGHEST = lax.Precision.HIGHEST
VMEM_LIMIT_BYTES = 56 * 1024 * 1024


def _cparams(*sem):
    return pltpu.CompilerParams(dimension_semantics=sem, vmem_limit_bytes=VMEM_LIMIT_BYTES)


def _dot(a, b):
    return jnp.dot(a, b, preferred_element_type=F32)


def _modnorm(x, gs, sh):
    ms = jnp.mean(x * x, axis=-1, keepdims=True)
    return x * lax.rsqrt(ms + EPS) * gs + sh


def _sigmoid(x):
    return 1.0 / (1.0 + jnp.exp(-x))


def _silu(x):
    return x * _sigmoid(x)


def _layer_norm(x, g, b):
    xc = x - jnp.mean(x, axis=-1, keepdims=True)
    return xc * lax.rsqrt(jnp.mean(xc * xc, axis=-1, keepdims=True) + EPS) * g + b


def _ada_kernel(c_ref, w_ref, b_ref, o_ref):
    o_ref[0] = jnp.dot(_silu(c_ref[...]), w_ref[0], precision=HIGHEST,
                       preferred_element_type=F32) + b_ref[0]


def _ada(cc, w_ada, b_ada):
    n, d = cc.shape
    cols = w_ada.shape[-1]
    tn = 1536
    return pl.pallas_call(
        _ada_kernel,
        grid=(DEPTH, cols // tn),
        in_specs=[pl.BlockSpec((n, d), lambda l, j: (0, 0)),
                  pl.BlockSpec((1, d, tn), lambda l, j: (l, 0, j)),
                  pl.BlockSpec((1, 1, tn), lambda l, j: (l, 0, j))],
        out_specs=pl.BlockSpec((1, n, tn), lambda l, j: (l, 0, j)),
        out_shape=jax.ShapeDtypeStruct((DEPTH, n, cols), F32),
        compiler_params=_cparams("parallel", "parallel"),
        name="ada",
    )(cc, w_ada, b_ada.reshape(DEPTH, 1, cols))


def _inproj_kernel(x_ref, gs_ref, sh_ref, w_ref, cos_ref, sin_ref,
                   qkvg_ref, conf_ref, gz_ref, ff_ref, hx_ref):
    hx_ref[...] = _modnorm(x_ref[0], gs_ref[0], sh_ref[0]).astype(BF16)
    cos = cos_ref[...]
    sin = sin_ref[...]
    half = BW // 2

    def rot(p, scale):
        p1, p2 = p[:, :half], p[:, half:]
        return jnp.concatenate([(p1 * cos - p2 * sin) * scale, (p1 * sin + p2 * cos) * scale], axis=-1)

    pq = _dot(hx_ref[...], w_ref[:, 0:BW])
    qkvg_ref[0, :, 0:BW] = rot(pq, 1.0).astype(BF16)
    pk = _dot(hx_ref[...], w_ref[:, BW:2 * BW])
    qkvg_ref[0, :, BW:2 * BW] = rot(pk, RET_HEAD_DIM ** -0.5).astype(BF16)
    pv = _dot(hx_ref[...], w_ref[:, 2 * BW:4 * BW])
    qkvg_ref[0, :, 2 * BW:4 * BW] = pv.astype(BF16)
    pc = _dot(hx_ref[...], w_ref[:, 4 * BW:6 * BW])
    conf_ref[0] = (pc[:, :BW] * _sigmoid(pc[:, BW:])).astype(BF16)
    gz_ref[0] = _dot(hx_ref[...], w_ref[:, 6 * BW:8 * BW]).astype(BF16)
    ff_ref[0] = _dot(hx_ref[...], w_ref[:, 8 * BW:9 * BW]).astype(BF16)


def _inproj(x, gs, sh, w_small, cos_t, sin_t):
    b, l, d = x.shape
    tm = min(l, 512)
    vec = pl.BlockSpec((1, 1, d), lambda i, j: (i, 0, 0))
    tab = pl.BlockSpec((tm, BW // 2), lambda i, j: (j, 0))

    def out(c):
        return pl.BlockSpec((1, tm, c), lambda i, j: (i, j, 0))

    return pl.pallas_call(
        _inproj_kernel,
        grid=(b, l // tm),
        in_specs=[pl.BlockSpec((1, tm, d), lambda i, j: (i, j, 0)), vec, vec,
                  pl.BlockSpec((d, SMALL_COLS), lambda i, j: (0, 0)), tab, tab],
        out_specs=[out(4 * BW), out(BW), out(2 * BW), out(BW)],
        out_shape=[jax.ShapeDtypeStruct((b, l, c), BF16) for c in (4 * BW, BW, 2 * BW, BW)],
        scratch_shapes=[pltpu.VMEM((tm, d), BF16)],
        compiler_params=_cparams("parallel", "parallel"),
        name="inproj",
    )(x, gs, sh, w_small, cos_t, sin_t)


def _mixer_kernel(n_chunks, qkvg_ref, conf_ref, gz_ref, sf0_ref, sb0_ref,
                  mst_ref, wqf_ref, wqb_ref, wkf_ref, wkb_ref, cdf_ref, cdb_ref, bd_ref, avg_ref, gn_ref, hm_ref,
                  cw_ref, cb_ref, cg_ref, cbe_ref, mg_ref, mbe_ref, mws_ref, mbs_ref, mgm_ref,
                  out_ref, cout_ref, mout_ref, sf_ref, sb_ref, o_scr, uf_scr, ub_scr, st_scr, hp_scr):
    unroll = 4 if n_chunks % 4 == 0 else (2 if n_chunks % 2 == 0 else 1)
    l = n_chunks * CHUNK
    hp_scr[0:CONV_PAD, :] = jnp.zeros((CONV_PAD, BW), F32)
    hp_scr[CONV_PAD + l:2 * CONV_PAD + l, :] = jnp.zeros((CONV_PAD, BW), F32)

    def chunk_rows(n):
        return pl.ds(pl.multiple_of(n * CHUNK, CHUNK), CHUNK)

    def state_update(k, wk_ref, v):
        kdec = (k * wk_ref[...]).astype(BF16)
        upd = lax.dot_general(kdec, v, (((0,), (0,)), ((), ())), preferred_element_type=F32)
        return bd_ref[...] * upd

    def intra(n, carry):
        rows = chunk_rows(n)
        q = qkvg_ref[0, rows, 0:BW]
        k = qkvg_ref[0, rows, BW:2 * BW]
        v = qkvg_ref[0, rows, 2 * BW:3 * BW]
        kf = k.astype(F32)
        qst = jnp.concatenate([q * hm_ref[h] for h in range(RET_HEADS)], axis=0)
        s = lax.dot_general(qst, k, (((1,), (1,)), ((), ())), preferred_element_type=F32)
        p = (s * mst_ref[...]).astype(BF16)
        pcat = jnp.concatenate([p[h * CHUNK:(h + 1) * CHUNK] for h in range(RET_HEADS)], axis=1)
        vbd = jnp.concatenate([v * hm_ref[RET_HEADS + h] for h in range(RET_HEADS)], axis=0)
        o_scr[rows, :] = _dot(pcat, vbd)
        uf_scr[n] = state_update(kf, wkf_ref, v)
        ub_scr[n] = state_update(kf, wkb_ref, v)
        mout_ref[0, rows, :] = _gmlp_chunk(gz_ref[0, rows, :], mg_ref, mbe_ref, mws_ref, mbs_ref,
                                           mgm_ref).astype(BF16)
        hp_scr[pl.ds(pl.multiple_of(n * CHUNK + CONV_PAD, 8), CHUNK), :] = conf_ref[0, rows, :].astype(F32)
        return carry

    lax.fori_loop(0, n_chunks, intra, 0, unroll=unroll)

    sf_ref[0] = sf0_ref[0]
    sb_ref[0] = sb0_ref[0]

    def scan(i, carry):
        st_scr[i, 0:BW, :] = sf_ref[0].astype(BF16)
        sf_ref[0] = cdf_ref[...] * sf_ref[0] + uf_scr[i]
        n = n_chunks - 1 - i
        st_scr[n, BW:2 * BW, :] = sb_ref[0].astype(BF16)
        sb_ref[0] = cdb_ref[...] * sb_ref[0] + ub_scr[n]
        return carry

    lax.fori_loop(0, n_chunks, scan, 0)

    def group_mean(t):
        hi = t.astype(BF16)
        lo = (t - hi.astype(F32)).astype(BF16)
        return _dot(jnp.concatenate([hi, lo], axis=1), avg_ref[...])

    def finish(n, carry):
        rows = chunk_rows(n)
        q = qkvg_ref[0, rows, 0:BW].astype(F32)
        qdec = jnp.concatenate([(q * wqf_ref[...]).astype(BF16), (q * wqb_ref[...]).astype(BF16)], axis=1)
        o = o_scr[rows, :] + _dot(qdec, st_scr[n])
        oc = o - group_mean(o)
        y = oc * lax.rsqrt(group_mean(oc * oc) + EPS) * gn_ref[...]
        g = qkvg_ref[0, rows, 3 * BW:4 * BW].astype(F32)
        out_ref[0, rows, :] = (_silu(g) * y).astype(BF16)
        for t in range(CHUNK // CONV_TILE):
            base = pl.multiple_of(n * CHUNK + t * CONV_TILE, CONV_TILE)
            cout_ref[0, pl.ds(base, CONV_TILE), :] = _conv_tile(hp_scr, base, cw_ref, cb_ref, cg_ref,
                                                                cbe_ref).astype(BF16)
        return carry

    lax.fori_loop(0, n_chunks, finish, 0, unroll=unroll)


def _mixers(qkvg, conf_h, gz, sf0, sb0, tabs, gn, conv_consts, gmlp_consts):
    b, l, _ = qkvg.shape
    n_chunks = l // CHUNK
    st = pl.BlockSpec((1, BW, BW), lambda i: (i, 0, 0))

    def seq(c):
        return pl.BlockSpec((1, l, c), lambda i: (i, 0, 0))

    def full(a):
        return pl.BlockSpec(a.shape, lambda i: (0,) * a.ndim)

    consts = ([tabs[k] for k in ("mst", "wqf", "wqb", "wkf", "wkb", "cdf", "cdb", "bd", "avg")] + [gn, tabs["hm"]]
              + conv_consts + gmlp_consts)
    return pl.pallas_call(
        functools.partial(_mixer_kernel, n_chunks),
        grid=(b,),
        in_specs=[seq(4 * BW), seq(BW), seq(2 * BW), st, st] + [full(a) for a in consts],
        out_specs=[seq(BW), seq(BW), seq(BW), st, st],
        out_shape=[jax.ShapeDtypeStruct((b, l, BW), BF16)] * 3 + [jax.ShapeDtypeStruct((b, BW, BW), F32)] * 2,
        scratch_shapes=[pltpu.VMEM((l, BW), F32),
                        pltpu.VMEM((n_chunks, BW, BW), F32),
                        pltpu.VMEM((n_chunks, BW, BW), F32),
                        pltpu.VMEM((n_chunks, 2 * BW, BW), BF16),
                        pltpu.VMEM((l + 2 * CONV_PAD, BW), F32)],
        compiler_params=_cparams("parallel"),
        name="mixers",
    )(qkvg, conf_h, gz, sf0, sb0, *consts)


def _state_kernel(n_chunks, k_ref, v_ref, wkf_ref, wkb_ref, cdf_ref, cdb_ref, bd_ref, sf_ref, sb_ref):
    sf_ref[0] = jnp.zeros((BW, BW), F32)
    sb_ref[0] = jnp.zeros((BW, BW), F32)

    def increment(n, wk_ref):
        rows = pl.ds(pl.multiple_of(n * CHUNK, CHUNK), CHUNK)
        kdec = (k_ref[0, rows, :].astype(F32) * wk_ref[...]).astype(BF16)
        upd = lax.dot_general(kdec, v_ref[0, rows, :], (((0,), (0,)), ((), ())), preferred_element_type=F32)
        return bd_ref[...] * upd

    def scan(i, carry):
        sf_ref[0] = cdf_ref[...] * sf_ref[0] + increment(i, wkf_ref)
        sb_ref[0] = cdb_ref[...] * sb_ref[0] + increment(n_chunks - 1 - i, wkb_ref)
        return carry

    lax.fori_loop(0, n_chunks, scan, 0)


def _final_states(qkvg, tabs):
    b, l, _ = qkvg.shape
    st = pl.BlockSpec((1, BW, BW), lambda i: (i, 0, 0))

    def full(a):
        return pl.BlockSpec(a.shape, lambda i: (0,) * a.ndim)

    consts = [tabs[k] for k in ("wkf", "wkb", "cdf", "cdb", "bd")]
    return pl.pallas_call(
        functools.partial(_state_kernel, l // CHUNK),
        grid=(b,),
        in_specs=[pl.BlockSpec((1, l, BW), lambda i: (i, 0, 1)),
                  pl.BlockSpec((1, l, BW), lambda i: (i, 0, 2))] + [full(a) for a in consts],
        out_specs=[st, st],
        out_shape=[jax.ShapeDtypeStruct((b, BW, BW), F32)] * 2,
        compiler_params=_cparams("parallel"),
        name="final_states",
    )(qkvg, qkvg, *consts)


def _retention_tables(log_g):
    lf, lb = log_g[0], log_g[1]
    idx = jnp.arange(CHUNK, dtype=F32)
    diff = idx[:, None] - idx[None, :]
    m = (jnp.where(diff > 0, jnp.exp(lf[:, None, None] * jnp.maximum(diff, 0.0)), 0.0)
         + jnp.where(diff < 0, jnp.exp(lb[:, None, None] * jnp.maximum(-diff, 0.0)), 0.0)
         + jnp.where(diff == 0, 2.0, 0.0))
    lane = np.arange(BW)
    hq = (lane % (BW // 2)) // (RET_HEAD_DIM // 2)
    hv = lane // RET_HEAD_DIM
    lfq, lbq = lf[hq][None, :], lb[hq][None, :]
    i = idx[:, None]
    hm = np.concatenate([(hq[None, :] == np.arange(RET_HEADS)[:, None]),
                         (hv[None, :] == np.arange(RET_HEADS)[:, None])], axis=0).astype(np.float32)
    return dict(
        mst=m.reshape(RET_HEADS * CHUNK, CHUNK),
        wqf=jnp.exp(lfq * (i + 1.0)), wqb=jnp.exp(lbq * (CHUNK - i)),
        wkf=jnp.exp(lfq * (CHUNK - 1.0 - i)), wkb=jnp.exp(lbq * i),
        cdf=jnp.broadcast_to(jnp.exp(lf[hq] * CHUNK)[:, None], (BW, BW)),
        cdb=jnp.broadcast_to(jnp.exp(lb[hq] * CHUNK)[:, None], (BW, BW)),
        bd=jnp.asarray((hq[:, None] == hv[None, :]).astype(np.float32)),
        avg=jnp.asarray(np.tile((hv[:, None] == hv[None, :]).astype(np.float32) / RET_HEAD_DIM,
                                (2, 1))).astype(BF16),
        hm=jnp.asarray(np.broadcast_to(hm[:, None, :], (2 * RET_HEADS, CHUNK, BW))).astype(BF16),
    )


CONV_TILE = 128
CONV_PAD = 16


def _conv_tile(hp_ref, base, w_ref, b_ref, g_ref, be_ref):
    first = CONV_PAD - (CONV_KERNEL - 1) // 2
    win = hp_ref[pl.ds(base, CONV_TILE + 2 * CONV_PAD), :]
    acc = jnp.zeros((CONV_TILE, BW), F32) + b_ref[...]
    for s in range(8):
        part = None
        for m in range(4):
            k = 8 * m + s - first
            if 0 <= k < CONV_KERNEL:
                term = win[8 * m:8 * m + CONV_TILE + 8, :] * w_ref[k:k + 1, :]
                part = term if part is None else part + term
        if part is not None:
            acc = acc + part[s:s + CONV_TILE, :]
    return _silu(_layer_norm(acc, g_ref[...], be_ref[...]))


def _conv_consts(w_dw, b_dw, ln_g, ln_b):
    return [w_dw, b_dw.reshape(1, BW), ln_g.reshape(1, BW), ln_b.reshape(1, BW)]


def _gmlp_chunk(z, g_ref, be_ref, ws_ref, bs_ref, gm_ref):
    z = z.astype(F32)
    z = 0.5 * z * (1.0 + lax.erf(z * (2.0 ** -0.5)))
    u, v = z[:, :BW], z[:, BW:]
    v = _layer_norm(v, g_ref[...], be_ref[...]).astype(BF16)
    full = _dot(ws_ref[...], v)
    sv = bs_ref[...]
    for g in range(GROUPS):
        sv = sv + full[g * CHUNK:(g + 1) * CHUNK, :] * gm_ref[g:g + 1, :]
    return u * sv


def _gmlp_consts(ln_g, ln_b, ws, bs):
    gm = np.repeat(np.eye(GROUPS, dtype=np.float32), GROUP_C, axis=1)
    return [ln_g.reshape(1, BW), ln_b.reshape(1, BW), ws.reshape(GROUPS * CHUNK, CHUNK).astype(BF16),
            jnp.repeat(bs.T, GROUP_C, axis=1), jnp.asarray(gm)]


FNET_RADIX = 4


@functools.lru_cache(maxsize=None)
def _dft_tables(l):
    n2 = l // FNET_RADIX
    c = np.arange(GROUP_C, dtype=np.int64)
    angc = 2.0 * np.pi * ((c[:, None] * c[None, :]) % GROUP_C) / GROUP_C
    eye = np.eye(GROUPS)
    wcs = np.concatenate([np.kron(eye, np.cos(angc)), np.kron(eye, np.sin(angc))], axis=1)
    n = np.arange(n2, dtype=np.int64)
    ang = 2.0 * np.pi * ((n[:, None] * n[None, :]) % n2) / n2
    cc, ss = np.cos(ang), np.sin(ang)
    m1 = np.block([[cc, ss], [-ss, cc]])
    angt = 2.0 * np.pi * (n[:, None] * np.arange(FNET_RADIX)[None, :]) / l
    tc = np.repeat(np.cos(angt), BW, axis=1)
    ts = np.repeat(np.sin(angt), BW, axis=1)
    return tuple(t.astype(np.float32) for t in (wcs, m1, tc, ts))


def _fnet_kernel(n2, scale, f_ref, wcs_ref, m1_ref, tc_ref, ts_ref, out_ref, ab_ref, zr_ref, zi_ref):
    for i in range(FNET_RADIX):
        cols = slice(i * BW, (i + 1) * BW)
        cs = _dot(f_ref[0, :, cols], wcs_ref[...])
        ab_ref[0:n2, cols] = cs[:, :BW].astype(BF16)
        ab_ref[n2:2 * n2, cols] = (-cs[:, BW:]).astype(BF16)
    for i in range(FNET_RADIX):
        cols = slice(i * BW, (i + 1) * BW)
        y = _dot(m1_ref[...], ab_ref[:, cols])
        yr, yi = y[:n2], y[n2:]
        if i == 0:
            zr_ref[:, cols] = yr
            zi_ref[:, cols] = yi
        else:
            tc, ts = tc_ref[:, cols], ts_ref[:, cols]
            zr_ref[:, cols] = yr * tc + yi * ts
            zi_ref[:, cols] = yi * tc - yr * ts
    step = min(n2, 128)
    for t in range(n2 // step):
        rows = slice(t * step, (t + 1) * step)
        zr = [zr_ref[rows, i * BW:(i + 1) * BW] for i in range(FNET_RADIX)]
        zi = [zi_ref[rows, i * BW:(i + 1) * BW] for i in range(FNET_RADIX)]
        even, odd = zr[0] + zr[2], zr[1] + zr[3]
        deven, dodd = zr[0] - zr[2], zi[1] - zi[3]
        for k1, val in enumerate((even + odd, deven + dodd, even - odd, deven - dodd)):
            out_ref[0, k1 * n2 + t * step:k1 * n2 + (t + 1) * step, :] = (val * scale).astype(BF16)


def _fnet(f):
    b, l, _ = f.shape
    n2 = l // FNET_RADIX
    wcs, m1, tc, ts = _dft_tables(l)
    scale = float(1.0 / np.sqrt(l * GROUP_C))

    def full(a):
        return pl.BlockSpec(a.shape, lambda i: (0,) * a.ndim)

    consts = [jnp.asarray(wcs).astype(BF16), jnp.asarray(m1).astype(BF16), jnp.asarray(tc), jnp.asarray(ts)]
    return pl.pallas_call(
        functools.partial(_fnet_kernel, n2, scale),
        grid=(b,),
        in_specs=[pl.BlockSpec((1, n2, FNET_RADIX * BW), lambda i: (i, 0, 0))] + [full(a) for a in consts],
        out_specs=pl.BlockSpec((1, l, BW), lambda i: (i, 0, 0)),
        out_shape=jax.ShapeDtypeStruct((b, l, BW), BF16),
        scratch_shapes=[pltpu.VMEM((2 * n2, FNET_RADIX * BW), BF16),
                        pltpu.VMEM((n2, FNET_RADIX * BW), F32),
                        pltpu.VMEM((n2, FNET_RADIX * BW), F32)],
        compiler_params=_cparams("parallel"),
        name="fnet",
    )(f.reshape(b, n2, FNET_RADIX * BW), *consts)


def _merge_kernel(x_ref, gs_ref, sh_ref, ga_ref, p0_ref, p1_ref, p2_ref, p3_ref,
                  wg_ref, bg_ref, wo_ref, wout_ref, out_ref, hx_ref, m_ref):
    x = x_ref[0]
    hx_ref[...] = _modnorm(x, gs_ref[0], sh_ref[0]).astype(BF16)
    for i, p_ref in enumerate((p0_ref, p1_ref, p2_ref, p3_ref)):
        cols = slice(i * D_MODEL, (i + 1) * D_MODEL)
        gate = _sigmoid(_dot(hx_ref[...], wg_ref[:, cols]) + bg_ref[:, cols])
        term = gate * _dot(p_ref[0], wo_ref[i])
        if i == 0:
            m_ref[...] = term
        else:
            m_ref[...] += term
    out_ref[0] = x + ga_ref[0] * _dot(m_ref[...].astype(BF16), wout_ref[...])


def _merge(x, gs, sh, ga, pres, w_gate, b_gate, w_bo, w_out):
    b, l, d = x.shape
    tm = min(l, 512)
    vec = pl.BlockSpec((1, 1, d), lambda i, j: (i, 0, 0))
    pre = pl.BlockSpec((1, tm, BW), lambda i, j: (i, j, 0))
    return pl.pallas_call(
        _merge_kernel,
        grid=(b, l // tm),
        in_specs=[pl.BlockSpec((1, tm, d), lambda i, j: (i, j, 0)), vec, vec, vec, pre, pre, pre, pre,
                  pl.BlockSpec((d, 4 * d), lambda i, j: (0, 0)),
                  pl.BlockSpec((1, 4 * d), lambda i, j: (0, 0)),
                  pl.BlockSpec((4, BW, d), lambda i, j: (0, 0, 0)),
                  pl.BlockSpec((d, d), lambda i, j: (0, 0))],
        out_specs=pl.BlockSpec((1, tm, d), lambda i, j: (i, j, 0)),
        out_shape=jax.ShapeDtypeStruct((b, l, d), F32),
        scratch_shapes=[pltpu.VMEM((tm, d), BF16), pltpu.VMEM((tm, d), F32)],
        compiler_params=_cparams("parallel", "parallel"),
        name="merge",
    )(x, gs, sh, ga, *pres, w_gate, b_gate.reshape(1, 4 * d), w_bo, w_out)


FFN_TC = 256
FFN_ROWS = 256
FFN_CONV_ROWS = 64
FFN_CONV_LANES = 128


def _ffn_pad(grid_w, row_conv):
    return grid_w if row_conv else 8


def _ffn_kernel(l, grid_w, row_conv, final_norm, x_ref, gs_ref, sh_ref, ga_ref, wa_ref, wb_ref, dw_ref,
                db_ref, wd_ref, gf_ref, out_ref, hx_ref, ac_ref, b_ref, h_ref, wds_ref):
    j = pl.program_id(1)
    pad = _ffn_pad(grid_w, row_conv)
    n_tiles = l // FFN_ROWS

    @pl.when(j == 0)
    def _():
        for t in range(n_tiles):
            rows = slice(t * FFN_ROWS, (t + 1) * FFN_ROWS)
            x = x_ref[0, rows, :]
            hx_ref[rows, :] = _modnorm(x, gs_ref[0], sh_ref[0]).astype(BF16)
            out_ref[0, rows, :] = x
        ac_ref[0:pad, :] = jnp.zeros((pad, FFN_TC), F32)
        ac_ref[pad + l:2 * pad + l, :] = jnp.zeros((pad, FFN_TC), F32)

    wds_ref[...] = (wd_ref[...].astype(F32) * ga_ref[0]).astype(BF16)

    step = min(l, 512)
    for t in range(l // step):
        rows = slice(t * step, (t + 1) * step)
        ac_ref[pad + t * step:pad + (t + 1) * step, :] = _dot(hx_ref[rows, :], wa_ref[...])
        b_ref[rows, :] = _dot(hx_ref[rows, :], wb_ref[...])

    tr = max(grid_w, FFN_CONV_ROWS)
    col = lax.broadcasted_iota(jnp.int32, (tr, FFN_CONV_LANES), 0) % grid_w
    not_first = col != 0
    not_last = col != grid_w - 1
    for t in range(l // tr):
        for c in range(FFN_TC // FFN_CONV_LANES):
            lanes = slice(c * FFN_CONV_LANES, (c + 1) * FFN_CONV_LANES)
            left = mid = right = None
            for dr in ((-1, 0, 1) if row_conv else (0,)):
                off = pad + t * tr + dr * grid_w
                tap = 3 * (dr + 1)
                a = ac_ref[off:off + tr, lanes]
                terms = [a * dw_ref[tap + k:tap + k + 1, lanes] for k in range(3)]
                left, mid, right = terms if left is None else (left + terms[0], mid + terms[1], right + terms[2])
            conv = (mid + db_ref[:, lanes]
                    + jnp.where(not_first, pltpu.roll(left, 1, 0), 0.0)
                    + jnp.where(not_last, pltpu.roll(right, tr - 1, 0), 0.0))
            rows = slice(t * tr, (t + 1) * tr)
            h_ref[rows, lanes] = (_silu(conv) * b_ref[rows, lanes]).astype(BF16)

    for t in range(l // step):
        rows = slice(t * step, (t + 1) * step)
        out_ref[0, rows, :] += _dot(h_ref[rows, :], wds_ref[...])

    if final_norm:
        @pl.when(j == pl.num_programs(1) - 1)
        def _():
            for t in range(l // FFN_ROWS):
                rows = slice(t * FFN_ROWS, (t + 1) * FFN_ROWS)
                y = out_ref[0, rows, :]
                out_ref[0, rows, :] = y * lax.rsqrt(jnp.mean(y * y, axis=-1, keepdims=True) + EPS) * gf_ref[...]


def _conv_ffn(x, gs, sh, ga, w_up, dw, db, w_down, g_final, grid_w, row_conv, final_norm):
    b, l, d = x.shape
    nj = D_FF // FFN_TC
    pad = _ffn_pad(grid_w, row_conv)
    vec = pl.BlockSpec((1, 1, d), lambda i, j: (i, 0, 0))
    return pl.pallas_call(
        functools.partial(_ffn_kernel, l, grid_w, row_conv, final_norm),
        grid=(b, nj),
        in_specs=[pl.BlockSpec((1, l, d), lambda i, j: (i, 0, 0)), vec, vec, vec,
                  pl.BlockSpec((d, FFN_TC), lambda i, j: (0, j)),
                  pl.BlockSpec((d, FFN_TC), lambda i, j: (0, j + nj)),
                  pl.BlockSpec((9, FFN_TC), lambda i, j: (0, j)),
                  pl.BlockSpec((1, FFN_TC), lambda i, j: (0, j)),
                  pl.BlockSpec((FFN_TC, d), lambda i, j: (j, 0)),
                  pl.BlockSpec((1, d), lambda i, j: (0, 0))],
        out_specs=pl.BlockSpec((1, l, d), lambda i, j: (i, 0, 0)),
        out_shape=jax.ShapeDtypeStruct((b, l, d), F32),
        scratch_shapes=[pltpu.VMEM((l, d), BF16), pltpu.VMEM((l + 2 * pad, FFN_TC), F32),
                        pltpu.VMEM((l, FFN_TC), F32), pltpu.VMEM((l, FFN_TC), BF16),
                        pltpu.VMEM((FFN_TC, d), BF16)],
        compiler_params=_cparams("parallel", "arbitrary"),
        name="conv_ffn",
    )(x, gs, sh, ga, w_up, w_up, dw.reshape(9, D_FF), db.reshape(1, D_FF), w_down, g_final.reshape(1, d))


def _rope_tables(n_pos):
    half = RET_HEAD_DIM // 2
    inv_freq = ROPE_BASE ** (-jnp.arange(half, dtype=F32) / half)
    ang = jnp.arange(n_pos, dtype=jnp.int32).astype(F32)[:, None] * inv_freq[None, :]
    return jnp.tile(jnp.cos(ang), (1, RET_HEADS)), jnp.tile(jnp.sin(ang), (1, RET_HEADS))


def _split_half_perm():
    half = RET_HEAD_DIM // 2
    t, h, i = np.meshgrid(np.arange(2), np.arange(RET_HEADS), np.arange(half), indexing="ij")
    return (h * RET_HEAD_DIM + t * half + i).reshape(-1)


def kernel(x, c, ctx, c_ctx, w_ada, b_ada, g_norm1, g_norm2, w_in, b_gate, ret_decay, ret_gn, w_ret_o, conv_dw, conv_db, conv_ln_g, conv_ln_b, w_conv_o, gmlp_ln_g, gmlp_ln_b, gmlp_ws, gmlp_bs, w_gmlp_o, w_fnet_o, w_out, w_ffn_up, ffn_dw, ffn_db, w_ffn_down, g_final):
    bsz, seq, d = x.shape
    n_ctx = ctx.shape[1]
    cc = jnp.concatenate([c, c_ctx[None, :], jnp.zeros((7, d), F32)], axis=0)
    mods = _ada(cc, w_ada, b_ada)
    cos_t, sin_t = _rope_tables(n_ctx + seq)
    perm = _split_half_perm()
    zero_state = jnp.zeros((bsz, BW, BW), F32)
    xc = ctx

    def mixers(stream, l_idx, gs, sh, ga, cos_p, sin_p, sf0, sb0, tabs, want_out):
        w = w_in[l_idx]
        w_small = jnp.concatenate([w[:, :BW][:, perm], w[:, BW:2 * BW][:, perm], w[:, 2 * BW:SMALL_COLS]],
                                  axis=1).astype(BF16)
        qkvg, conf_h, gz, ff = _inproj(stream, gs, sh, w_small, cos_p, sin_p)
        if not want_out:
            return (None,) + tuple(_final_states(qkvg, tabs))
        ret_pre, conf_pre, gm_pre, sf, sb = _mixers(
            qkvg, conf_h, gz, sf0, sb0, tabs, ret_gn[l_idx].reshape(1, BW),
            _conv_consts(conv_dw[l_idx], conv_db[l_idx], conv_ln_g[l_idx], conv_ln_b[l_idx]),
            _gmlp_consts(gmlp_ln_g[l_idx], gmlp_ln_b[l_idx], gmlp_ws[l_idx], gmlp_bs[l_idx]))
        fn_pre = _fnet(ff)
        w_bo = jnp.stack([w_ret_o[l_idx], w_conv_o[l_idx], w_gmlp_o[l_idx], w_fnet_o[l_idx]]).astype(BF16)
        new = _merge(stream, gs, sh, ga, (ret_pre, conf_pre, gm_pre, fn_pre),
                     w[:, GATE_OFF:].astype(BF16), b_gate[l_idx], w_bo, w_out[l_idx].astype(BF16))
        return new, sf, sb

    for l_idx in range(DEPTH):
        last = l_idx == DEPTH - 1
        lat = [m[:, None, :] for m in jnp.split(mods[l_idx, :bsz], 6, axis=-1)]
        cm = [jnp.broadcast_to(m[None, None, :], (bsz, 1, d)) for m in jnp.split(mods[l_idx, bsz], 6, axis=-1)]
        g1 = g_norm1[l_idx][None, None, :]
        g2 = g_norm2[l_idx][None, None, :]
        tabs = _retention_tables(jax.nn.log_sigmoid(ret_decay[l_idx].astype(F32)))
        w_up = w_ffn_up[l_idx].astype(BF16)
        w_down = w_ffn_down[l_idx].astype(BF16)

        xc_mix, s_f, s_b = mixers(xc, l_idx, g1 * (1 + cm[1]), cm[0], cm[2], cos_t[:n_ctx], sin_t[:n_ctx],
                                  zero_state, zero_state, tabs, not last)
        x, _, _ = mixers(x, l_idx, g1 * (1 + lat[1]), lat[0], lat[2], cos_t[n_ctx:], sin_t[n_ctx:],
                         s_f, s_b, tabs, True)
        x = _conv_ffn(x, g2 * (1 + lat[4]), lat[3], lat[5], w_up, ffn_dw[l_idx], ffn_db[l_idx], w_down,
                      g_final, GRID_W, True, last)
        if not last:
            fold = max(1, seq // n_ctx)
            while bsz % fold:
                fold //= 2
            xc = _conv_ffn(xc_mix.reshape(bsz // fold, fold * n_ctx, d), (g2 * (1 + cm[4]))[:bsz // fold],
                           cm[3][:bsz // fold], cm[5][:bsz // fold], w_up, ffn_dw[l_idx], ffn_db[l_idx],
                           w_down, g_final, n_ctx, False, False).reshape(bsz, n_ctx, d)
    return x
```

```python
import functools

import jax
import jax.numpy as jnp
import numpy as np
from jax import lax
from jax.experimental import pallas as pl
from jax.experimental.pallas import tpu as pltpu

D_MODEL = 1024
DEPTH = 2
GRID_W = 64
RET_HEADS = 4
RET_HEAD_DIM = 64
BW = 256
CHUNK = 128
LANES = 128
ROPE_BASE = 10000.0
CONV_KERNEL = 31
GROUPS = 4
GROUP_C = BW // GROUPS
D_FF = ((8 * D_MODEL // 3 + 127) // 128) * 128
EPS = 1e-6
SMALL_COLS = 9 * BW
GATE_OFF = SMALL_COLS

F32 = jnp.float32
BF16 = jnp.bfloat16
HIGHEST = lax.Precision.HIGHEST
VMEM_LIMIT_BYTES = 56 * 1024 * 1024


def _cparams(*sem):
    return pltpu.CompilerParams(dimension_semantics=sem, vmem_limit_bytes=VMEM_LIMIT_BYTES)


def _dot(a, b):
    return jnp.dot(a, b, preferred_element_type=F32)


def _modnorm(x, gs, sh):
    ms = jnp.mean(x * x, axis=-1, keepdims=True)
    return x * lax.rsqrt(ms + EPS) * gs + sh


def _sigmoid(x):
    return 1.0 / (1.0 + jnp.exp(-x))


def _silu(x):
    return x * _sigmoid(x)


def _layer_norm(x, g, b):
    xc = x - jnp.mean(x, axis=-1, keepdims=True)
    return xc * lax.rsqrt(jnp.mean(xc * xc, axis=-1, keepdims=True) + EPS) * g + b


def _ada_kernel(c_ref, w_ref, b_ref, o_ref):
    o_ref[0] = jnp.dot(_silu(c_ref[...]), w_ref[0], precision=HIGHEST,
                       preferred_element_type=F32) + b_ref[0]


def _ada(cc, w_ada, b_ada):
    n, d = cc.shape
    cols = w_ada.shape[-1]
    tn = 1536
    return pl.pallas_call(
        _ada_kernel,
        grid=(DEPTH, cols // tn),
        in_specs=[pl.BlockSpec((n, d), lambda l, j: (0, 0)),
                  pl.BlockSpec((1, d, tn), lambda l, j: (l, 0, j)),
                  pl.BlockSpec((1, 1, tn), lambda l, j: (l, 0, j))],
        out_specs=pl.BlockSpec((1, n, tn), lambda l, j: (l, 0, j)),
        out_shape=jax.ShapeDtypeStruct((DEPTH, n, cols), F32),
        compiler_params=_cparams("parallel", "parallel"),
        name="ada",
    )(cc, w_ada, b_ada.reshape(DEPTH, 1, cols))


def _inproj_kernel(x_ref, gs_ref, sh_ref, w_ref, cos_ref, sin_ref,
                   qkvg_ref, conf_ref, gz_ref, ff_ref, hx_ref, pf_ref):
    hx_ref[...] = _modnorm(x_ref[0], gs_ref[0], sh_ref[0]).astype(BF16)
    cos = cos_ref[...]
    sin = sin_ref[...]
    half = BW // 2

    def rot(p, scale):
        p1, p2 = p[:, :half], p[:, half:]
        return jnp.concatenate([(p1 * cos - p2 * sin) * scale, (p1 * sin + p2 * cos) * scale], axis=-1)

    pq = _dot(hx_ref[...], w_ref[:, 0:BW])
    qkvg_ref[0, :, 0:BW] = rot(pq, 1.0).astype(BF16)
    pk = _dot(hx_ref[...], w_ref[:, BW:2 * BW])
    qkvg_ref[0, :, BW:2 * BW] = rot(pk, RET_HEAD_DIM ** -0.5).astype(BF16)
    pv = _dot(hx_ref[...], w_ref[:, 2 * BW:4 * BW])
    qkvg_ref[0, :, 2 * BW:4 * BW] = pv.astype(BF16)
    pc = _dot(hx_ref[...], w_ref[:, 4 * BW:6 * BW])
    conf_ref[0] = (pc[:, :BW] * _sigmoid(pc[:, BW:])).astype(BF16)
    gz_ref[0] = _dot(hx_ref[...], w_ref[:, 6 * BW:8 * BW]).astype(BF16)
    pf = _dot(hx_ref[...], w_ref[:, 8 * BW:9 * BW])
    rows = pf.shape[0] // FNET_RADIX
    for c in range(BW // LANES):
        pf_ref[c] = pf[:, c * LANES:(c + 1) * LANES]
        for n1 in range(FNET_RADIX):
            ff_ref[0, :, n1 * BW + c * LANES:n1 * BW + (c + 1) * LANES] = (
                pf_ref[c, pl.ds(n1, rows, stride=FNET_RADIX), :].astype(BF16))


def _inproj(x, gs, sh, w_small, cos_t, sin_t):
    b, l, d = x.shape
    tm = min(l, 512)
    vec = pl.BlockSpec((1, 1, d), lambda i, j: (i, 0, 0))
    tab = pl.BlockSpec((tm, BW // 2), lambda i, j: (j, 0))

    def out(c):
        return pl.BlockSpec((1, tm, c), lambda i, j: (i, j, 0))

    return pl.pallas_call(
        _inproj_kernel,
        grid=(b, l // tm),
        in_specs=[pl.BlockSpec((1, tm, d), lambda i, j: (i, j, 0)), vec, vec,
                  pl.BlockSpec((d, SMALL_COLS), lambda i, j: (0, 0)), tab, tab],
        out_specs=[out(4 * BW), out(BW), out(2 * BW),
                   pl.BlockSpec((1, tm // FNET_RADIX, FNET_RADIX * BW), lambda i, j: (i, j, 0))],
        out_shape=[jax.ShapeDtypeStruct((b, l, c), BF16) for c in (4 * BW, BW, 2 * BW)]
        + [jax.ShapeDtypeStruct((b, l // FNET_RADIX, FNET_RADIX * BW), BF16)],
        scratch_shapes=[pltpu.VMEM((tm, d), BF16), pltpu.VMEM((BW // LANES, tm, LANES), F32)],
        compiler_params=_cparams("parallel", "parallel"),
        name="inproj",
    )(x, gs, sh, w_small, cos_t, sin_t)


def _mixer_kernel(n_chunks, qkvg_ref, conf_ref, gz_ref, sf0_ref, sb0_ref,
                  mst_ref, wqf_ref, wqb_ref, wkf_ref, wkb_ref, cdf_ref, cdb_ref, bd_ref, avg_ref, gn_ref, hm_ref,
                  cw_ref, cb_ref, cg_ref, cbe_ref, mg_ref, mbe_ref, mws_ref, mbs_ref, mgm_ref,
                  out_ref, cout_ref, mout_ref, sf_ref, sb_ref, o_scr, uf_scr, ub_scr, st_scr, hp_scr):
    unroll = 4 if n_chunks % 4 == 0 else (2 if n_chunks % 2 == 0 else 1)
    l = n_chunks * CHUNK
    hp_scr[0:CONV_PAD, :] = jnp.zeros((CONV_PAD, BW), F32)
    hp_scr[CONV_PAD + l:2 * CONV_PAD + l, :] = jnp.zeros((CONV_PAD, BW), F32)

    def chunk_rows(n):
        return pl.ds(pl.multiple_of(n * CHUNK, CHUNK), CHUNK)

    def state_update(k, wk_ref, v):
        kdec = (k * wk_ref[...]).astype(BF16)
        upd = lax.dot_general(kdec, v, (((0,), (0,)), ((), ())), preferred_element_type=F32)
        return bd_ref[...] * upd

    def intra(n, carry):
        rows = chunk_rows(n)
        q = qkvg_ref[0, rows, 0:BW]
        k = qkvg_ref[0, rows, BW:2 * BW]
        v = qkvg_ref[0, rows, 2 * BW:3 * BW]
        kf = k.astype(F32)
        qst = jnp.concatenate([q * hm_ref[h] for h in range(RET_HEADS)], axis=0)
        s = lax.dot_general(qst, k, (((1,), (1,)), ((), ())), preferred_element_type=F32)
        p = (s * mst_ref[...]).astype(BF16)
        pcat = jnp.concatenate([p[h * CHUNK:(h + 1) * CHUNK] for h in range(RET_HEADS)], axis=1)
        vbd = jnp.concatenate([v * hm_ref[RET_HEADS + h] for h in range(RET_HEADS)], axis=0)
        o_scr[rows, :] = _dot(pcat, vbd)
        uf_scr[n] = state_update(kf, wkf_ref, v)
        ub_scr[n] = state_update(kf, wkb_ref, v)
        mout_ref[0, rows, :] = _gmlp_chunk(gz_ref[0, rows, :], mg_ref, mbe_ref, mws_ref, mbs_ref,
                                           mgm_ref).astype(BF16)
        hp_scr[pl.ds(pl.multiple_of(n * CHUNK + CONV_PAD, 8), CHUNK), :] = conf_ref[0, rows, :].astype(F32)
        return carry

    lax.fori_loop(0, n_chunks, intra, 0, unroll=unroll)

    sf_ref[0] = sf0_ref[0]
    sb_ref[0] = sb0_ref[0]

    def scan(i, carry):
        st_scr[i, 0:BW, :] = sf_ref[0].astype(BF16)
        sf_ref[0] = cdf_ref[...] * sf_ref[0] + uf_scr[i]
        n = n_chunks - 1 - i
        st_scr[n, BW:2 * BW, :] = sb_ref[0].astype(BF16)
        sb_ref[0] = cdb_ref[...] * sb_ref[0] + ub_scr[n]
        return carry

    lax.fori_loop(0, n_chunks, scan, 0)

    def group_mean(t):
        hi = t.astype(BF16)
        lo = (t - hi.astype(F32)).astype(BF16)
        return _dot(jnp.concatenate([hi, lo], axis=1), avg_ref[...])

    def finish(n, carry):
        rows = chunk_rows(n)
        q = qkvg_ref[0, rows, 0:BW].astype(F32)
        qdec = jnp.concatenate([(q * wqf_ref[...]).astype(BF16), (q * wqb_ref[...]).astype(BF16)], axis=1)
        o = o_scr[rows, :] + _dot(qdec, st_scr[n])
        oc = o - group_mean(o)
        y = oc * lax.rsqrt(group_mean(oc * oc) + EPS) * gn_ref[...]
        g = qkvg_ref[0, rows, 3 * BW:4 * BW].astype(F32)
        out_ref[0, rows, :] = (_silu(g) * y).astype(BF16)
        for t in range(CHUNK // CONV_TILE):
            base = pl.multiple_of(n * CHUNK + t * CONV_TILE, CONV_TILE)
            cout_ref[0, pl.ds(base, CONV_TILE), :] = _conv_tile(hp_scr, base, cw_ref, cb_ref, cg_ref,
                                                                cbe_ref).astype(BF16)
        return carry

    lax.fori_loop(0, n_chunks, finish, 0, unroll=unroll)


def _mixers(qkvg, conf_h, gz, sf0, sb0, tabs, gn, conv_consts, gmlp_consts):
    b, l, _ = qkvg.shape
    n_chunks = l // CHUNK
    st = pl.BlockSpec((1, BW, BW), lambda i: (i, 0, 0))

    def seq(c):
        return pl.BlockSpec((1, l, c), lambda i: (i, 0, 0))

    def full(a):
        return pl.BlockSpec(a.shape, lambda i: (0,) * a.ndim)

    consts = ([tabs[k] for k in ("mst", "wqf", "wqb", "wkf", "wkb", "cdf", "cdb", "bd", "avg")] + [gn, tabs["hm"]]
              + conv_consts + gmlp_consts)
    return pl.pallas_call(
        functools.partial(_mixer_kernel, n_chunks),
        grid=(b,),
        in_specs=[seq(4 * BW), seq(BW), seq(2 * BW), st, st] + [full(a) for a in consts],
        out_specs=[seq(BW), seq(BW), seq(BW), st, st],
        out_shape=[jax.ShapeDtypeStruct((b, l, BW), BF16)] * 3 + [jax.ShapeDtypeStruct((b, BW, BW), F32)] * 2,
        scratch_shapes=[pltpu.VMEM((l, BW), F32),
                        pltpu.VMEM((n_chunks, BW, BW), F32),
                        pltpu.VMEM((n_chunks, BW, BW), F32),
                        pltpu.VMEM((n_chunks, 2 * BW, BW), BF16),
                        pltpu.VMEM((l + 2 * CONV_PAD, BW), F32)],
        compiler_params=_cparams("parallel"),
        name="mixers",
    )(qkvg, conf_h, gz, sf0, sb0, *consts)


def _state_kernel(n_chunks, k_ref, v_ref, wkf_ref, wkb_ref, cdf_ref, cdb_ref, bd_ref, sf_ref, sb_ref):
    sf_ref[0] = jnp.zeros((BW, BW), F32)
    sb_ref[0] = jnp.zeros((BW, BW), F32)

    def increment(n, wk_ref):
        rows = pl.ds(pl.multiple_of(n * CHUNK, CHUNK), CHUNK)
        kdec = (k_ref[0, rows, :].astype(F32) * wk_ref[...]).astype(BF16)
        upd = lax.dot_general(kdec, v_ref[0, rows, :], (((0,), (0,)), ((), ())), preferred_element_type=F32)
        return bd_ref[...] * upd

    def scan(i, carry):
        sf_ref[0] = cdf_ref[...] * sf_ref[0] + increment(i, wkf_ref)
        sb_ref[0] = cdb_ref[...] * sb_ref[0] + increment(n_chunks - 1 - i, wkb_ref)
        return carry

    lax.fori_loop(0, n_chunks, scan, 0)


def _final_states(qkvg, tabs):
    b, l, _ = qkvg.shape
    st = pl.BlockSpec((1, BW, BW), lambda i: (i, 0, 0))

    def full(a):
        return pl.BlockSpec(a.shape, lambda i: (0,) * a.ndim)

    consts = [tabs[k] for k in ("wkf", "wkb", "cdf", "cdb", "bd")]
    return pl.pallas_call(
        functools.partial(_state_kernel, l // CHUNK),
        grid=(b,),
        in_specs=[pl.BlockSpec((1, l, BW), lambda i: (i, 0, 1)),
                  pl.BlockSpec((1, l, BW), lambda i: (i, 0, 2))] + [full(a) for a in consts],
        out_specs=[st, st],
        out_shape=[jax.ShapeDtypeStruct((b, BW, BW), F32)] * 2,
        compiler_params=_cparams("parallel"),
        name="final_states",
    )(qkvg, qkvg, *consts)


def _retention_tables(log_g):
    lf, lb = log_g[0], log_g[1]
    idx = jnp.arange(CHUNK, dtype=F32)
    diff = idx[:, None] - idx[None, :]
    m = (jnp.where(diff > 0, jnp.exp(lf[:, None, None] * jnp.maximum(diff, 0.0)), 0.0)
         + jnp.where(diff < 0, jnp.exp(lb[:, None, None] * jnp.maximum(-diff, 0.0)), 0.0)
         + jnp.where(diff == 0, 2.0, 0.0))
    lane = np.arange(BW)
    hq = (lane % (BW // 2)) // (RET_HEAD_DIM // 2)
    hv = lane // RET_HEAD_DIM
    lfq, lbq = lf[hq][None, :], lb[hq][None, :]
    i = idx[:, None]
    hm = np.concatenate([(hq[None, :] == np.arange(RET_HEADS)[:, None]),
                         (hv[None, :] == np.arange(RET_HEADS)[:, None])], axis=0).astype(np.float32)
    return dict(
        mst=m.reshape(RET_HEADS * CHUNK, CHUNK),
        wqf=jnp.exp(lfq * (i + 1.0)), wqb=jnp.exp(lbq * (CHUNK - i)),
        wkf=jnp.exp(lfq * (CHUNK - 1.0 - i)), wkb=jnp.exp(lbq * i),
        cdf=jnp.broadcast_to(jnp.exp(lf[hq] * CHUNK)[:, None], (BW, BW)),
        cdb=jnp.broadcast_to(jnp.exp(lb[hq] * CHUNK)[:, None], (BW, BW)),
        bd=jnp.asarray((hq[:, None] == hv[None, :]).astype(np.float32)),
        avg=jnp.asarray(np.tile((hv[:, None] == hv[None, :]).astype(np.float32) / RET_HEAD_DIM,
                                (2, 1))).astype(BF16),
        hm=jnp.asarray(np.broadcast_to(hm[:, None, :], (2 * RET_HEADS, CHUNK, BW))).astype(BF16),
    )


CONV_TILE = 128
CONV_PAD = 16


def _conv_tile(hp_ref, base, w_ref, b_ref, g_ref, be_ref):
    first = CONV_PAD - (CONV_KERNEL - 1) // 2
    win = hp_ref[pl.ds(base, CONV_TILE + 2 * CONV_PAD), :]
    acc = jnp.zeros((CONV_TILE, BW), F32) + b_ref[...]
    for s in range(8):
        part = None
        for m in range(4):
            k = 8 * m + s - first
            if 0 <= k < CONV_KERNEL:
                term = win[8 * m:8 * m + CONV_TILE + 8, :] * w_ref[k:k + 1, :]
                part = term if part is None else part + term
        if part is not None:
            acc = acc + part[s:s + CONV_TILE, :]
    return _silu(_layer_norm(acc, g_ref[...], be_ref[...]))


def _conv_consts(w_dw, b_dw, ln_g, ln_b):
    return [w_dw, b_dw.reshape(1, BW), ln_g.reshape(1, BW), ln_b.reshape(1, BW)]


def _gmlp_chunk(z, g_ref, be_ref, ws_ref, bs_ref, gm_ref):
    z = z.astype(F32)
    z = 0.5 * z * (1.0 + lax.erf(z * (2.0 ** -0.5)))
    u, v = z[:, :BW], z[:, BW:]
    v = _layer_norm(v, g_ref[...], be_ref[...]).astype(BF16)
    full = _dot(ws_ref[...], v)
    sv = bs_ref[...]
    for g in range(GROUPS):
        sv = sv + full[g * CHUNK:(g + 1) * CHUNK, :] * gm_ref[g:g + 1, :]
    return u * sv


def _gmlp_consts(ln_g, ln_b, ws, bs):
    gm = np.repeat(np.eye(GROUPS, dtype=np.float32), GROUP_C, axis=1)
    return [ln_g.reshape(1, BW), ln_b.reshape(1, BW), ws.reshape(GROUPS * CHUNK, CHUNK).astype(BF16),
            jnp.repeat(bs.T, GROUP_C, axis=1), jnp.asarray(gm)]


FNET_RADIX = 4


@functools.lru_cache(maxsize=None)
def _dft_tables(l):
    n2 = l // FNET_RADIX
    c = np.arange(GROUP_C, dtype=np.int64)
    angc = 2.0 * np.pi * ((c[:, None] * c[None, :]) % GROUP_C) / GROUP_C
    eye = np.eye(GROUPS)
    wcs = np.concatenate([np.kron(eye, np.cos(angc)), np.kron(eye, np.sin(angc))], axis=1)
    n = np.arange(n2, dtype=np.int64)
    ang = 2.0 * np.pi * ((n[:, None] * n[None, :]) % n2) / n2
    cc, ss = np.cos(ang), np.sin(ang)
    m1 = np.block([[cc, ss], [-ss, cc]])
    angt = 2.0 * np.pi * (n[:, None] * np.arange(FNET_RADIX)[None, :]) / l
    tc = np.repeat(np.cos(angt), BW, axis=1)
    ts = np.repeat(np.sin(angt), BW, axis=1)
    return tuple(t.astype(np.float32) for t in (wcs, m1, tc, ts))


def _fnet_kernel(n2, scale, f_ref, wcs_ref, m1_ref, tc_ref, ts_ref, out_ref, ab_ref, zr_ref, zi_ref):
    for i in range(FNET_RADIX):
        cols = slice(i * BW, (i + 1) * BW)
        cs = _dot(f_ref[0, :, cols], wcs_ref[...])
        ab_ref[0:n2, cols] = cs[:, :BW].astype(BF16)
        ab_ref[n2:2 * n2, cols] = (-cs[:, BW:]).astype(BF16)
    for i in range(FNET_RADIX):
        cols = slice(i * BW, (i + 1) * BW)
        y = _dot(m1_ref[...], ab_ref[:, cols])
        yr, yi = y[:n2], y[n2:]
        if i == 0:
            zr_ref[:, cols] = yr
            zi_ref[:, cols] = yi
        else:
            tc, ts = tc_ref[:, cols], ts_ref[:, cols]
            zr_ref[:, cols] = yr * tc + yi * ts
            zi_ref[:, cols] = yi * tc - yr * ts
    step = min(n2, 128)
    for t in range(n2 // step):
        rows = slice(t * step, (t + 1) * step)
        zr = [zr_ref[rows, i * BW:(i + 1) * BW] for i in range(FNET_RADIX)]
        zi = [zi_ref[rows, i * BW:(i + 1) * BW] for i in range(FNET_RADIX)]
        even, odd = zr[0] + zr[2], zr[1] + zr[3]
        deven, dodd = zr[0] - zr[2], zi[1] - zi[3]
        for k1, val in enumerate((even + odd, deven + dodd, even - odd, deven - dodd)):
            out_ref[0, k1 * n2 + t * step:k1 * n2 + (t + 1) * step, :] = (val * scale).astype(BF16)


def _fnet(f):
    b, n2, _ = f.shape
    l = n2 * FNET_RADIX
    wcs, m1, tc, ts = _dft_tables(l)
    scale = float(1.0 / np.sqrt(l * GROUP_C))

    def full(a):
        return pl.BlockSpec(a.shape, lambda i: (0,) * a.ndim)

    consts = [jnp.asarray(wcs).astype(BF16), jnp.asarray(m1).astype(BF16), jnp.asarray(tc), jnp.asarray(ts)]
    return pl.pallas_call(
        functools.partial(_fnet_kernel, n2, scale),
        grid=(b,),
        in_specs=[pl.BlockSpec((1, n2, FNET_RADIX * BW), lambda i: (i, 0, 0))] + [full(a) for a in consts],
        out_specs=pl.BlockSpec((1, l, BW), lambda i: (i, 0, 0)),
        out_shape=jax.ShapeDtypeStruct((b, l, BW), BF16),
        scratch_shapes=[pltpu.VMEM((2 * n2, FNET_RADIX * BW), BF16),
                        pltpu.VMEM((n2, FNET_RADIX * BW), F32),
                        pltpu.VMEM((n2, FNET_RADIX * BW), F32)],
        compiler_params=_cparams("parallel"),
        name="fnet",
    )(f, *consts)


def _merge_kernel(x_ref, gs_ref, sh_ref, ga_ref, p0_ref, p1_ref, p2_ref, p3_ref,
                  wg_ref, bg_ref, wo_ref, wout_ref, out_ref, hx_ref, m_ref):
    x = x_ref[0]
    hx_ref[...] = _modnorm(x, gs_ref[0], sh_ref[0]).astype(BF16)
    for i, p_ref in enumerate((p0_ref, p1_ref, p2_ref, p3_ref)):
        cols = slice(i * D_MODEL, (i + 1) * D_MODEL)
        gate = _sigmoid(_dot(hx_ref[...], wg_ref[:, cols]) + bg_ref[:, cols])
        term = gate * _dot(p_ref[0], wo_ref[i])
        if i == 0:
            m_ref[...] = term
        else:
            m_ref[...] += term
    out_ref[0] = x + ga_ref[0] * _dot(m_ref[...].astype(BF16), wout_ref[...])


def _merge(x, gs, sh, ga, pres, w_gate, b_gate, w_bo, w_out):
    b, l, d = x.shape
    tm = min(l, 512)
    vec = pl.BlockSpec((1, 1, d), lambda i, j: (i, 0, 0))
    pre = pl.BlockSpec((1, tm, BW), lambda i, j: (i, j, 0))
    return pl.pallas_call(
        _merge_kernel,
        grid=(b, l // tm),
        in_specs=[pl.BlockSpec((1, tm, d), lambda i, j: (i, j, 0)), vec, vec, vec, pre, pre, pre, pre,
                  pl.BlockSpec((d, 4 * d), lambda i, j: (0, 0)),
                  pl.BlockSpec((1, 4 * d), lambda i, j: (0, 0)),
                  pl.BlockSpec((4, BW, d), lambda i, j: (0, 0, 0)),
                  pl.BlockSpec((d, d), lambda i, j: (0, 0))],
        out_specs=pl.BlockSpec((1, tm, d), lambda i, j: (i, j, 0)),
        out_shape=jax.ShapeDtypeStruct((b, l, d), F32),
        scratch_shapes=[pltpu.VMEM((tm, d), BF16), pltpu.VMEM((tm, d), F32)],
        compiler_params=_cparams("parallel", "parallel"),
        name="merge",
    )(x, gs, sh, ga, *pres, w_gate, b_gate.reshape(1, 4 * d), w_bo, w_out)


FFN_TC = 256
FFN_ROWS = 256
FFN_SUB = 64


def _ffn_pad(grid_w, row_conv):
    return (grid_w if row_conv else 0) + 8


def _ffn_kernel(l, grid_w, row_conv, final_norm, x_ref, gs_ref, sh_ref, ga_ref, wa_ref, wb_ref, dw_ref,
                db_ref, wd_ref, gf_ref, out_ref, hx_ref, ac_ref, al_ref, ar_ref, b_ref, h_ref, wds_ref):
    j = pl.program_id(1)
    pad = _ffn_pad(grid_w, row_conv)
    n_tiles = l // FFN_ROWS

    @pl.when(j == 0)
    def _():
        for t in range(n_tiles):
            rows = slice(t * FFN_ROWS, (t + 1) * FFN_ROWS)
            x = x_ref[0, rows, :]
            hx_ref[rows, :] = _modnorm(x, gs_ref[0], sh_ref[0]).astype(BF16)
            out_ref[0, rows, :] = x
        for ref in (ac_ref, al_ref, ar_ref):
            ref[0:pad, :] = jnp.zeros((pad, FFN_TC), F32)
            ref[pad + l:2 * pad + l, :] = jnp.zeros((pad, FFN_TC), F32)

    wds_ref[...] = (wd_ref[...].astype(F32) * ga_ref[0]).astype(BF16)

    col = lax.broadcasted_iota(jnp.int32, (FFN_ROWS, FFN_TC), 0) % grid_w
    not_first = col != 0
    not_last = col != grid_w - 1

    step = min(l, 512)
    for t in range(l // step):
        rows = slice(t * step, (t + 1) * step)
        ac_ref[pad + t * step:pad + (t + 1) * step, :] = _dot(hx_ref[rows, :], wa_ref[...])
        b_ref[rows, :] = _dot(hx_ref[rows, :], wb_ref[...])

    for t in range(n_tiles):
        lo = pad + t * FFN_ROWS
        al_ref[lo:lo + FFN_ROWS, :] = jnp.where(not_first, ac_ref[lo - 1:lo - 1 + FFN_ROWS, :], 0.0)
        ar_ref[lo:lo + FFN_ROWS, :] = jnp.where(not_last, ac_ref[lo + 1:lo + 1 + FFN_ROWS, :], 0.0)

    for t in range(l // FFN_SUB):
        acc = jnp.zeros((FFN_SUB, FFN_TC), F32) + db_ref[...]
        for dr in ((-1, 0, 1) if row_conv else (0,)):
            off = pad + t * FFN_SUB + dr * grid_w
            tap = 3 * (dr + 1)
            acc = acc + al_ref[off:off + FFN_SUB, :] * dw_ref[tap:tap + 1, :]
            acc = acc + ac_ref[off:off + FFN_SUB, :] * dw_ref[tap + 1:tap + 2, :]
            acc = acc + ar_ref[off:off + FFN_SUB, :] * dw_ref[tap + 2:tap + 3, :]
        rows = slice(t * FFN_SUB, (t + 1) * FFN_SUB)
        h_ref[rows, :] = (_silu(acc) * b_ref[rows, :]).astype(BF16)

    for t in range(l // step):
        rows = slice(t * step, (t + 1) * step)
        out_ref[0, rows, :] += _dot(h_ref[rows, :], wds_ref[...])

    if final_norm:
        @pl.when(j == pl.num_programs(1) - 1)
        def _():
            for t in range(l // FFN_ROWS):
                rows = slice(t * FFN_ROWS, (t + 1) * FFN_ROWS)
                y = out_ref[0, rows, :]
                out_ref[0, rows, :] = y * lax.rsqrt(jnp.mean(y * y, axis=-1, keepdims=True) + EPS) * gf_ref[...]


def _conv_ffn(x, gs, sh, ga, w_up, dw, db, w_down, g_final, grid_w, row_conv, final_norm):
    b, l, d = x.shape
    nj = D_FF // FFN_TC
    pad = _ffn_pad(grid_w, row_conv)
    vec = pl.BlockSpec((1, 1, d), lambda i, j: (i, 0, 0))
    return pl.pallas_call(
        functools.partial(_ffn_kernel, l, grid_w, row_conv, final_norm),
        grid=(b, nj),
        in_specs=[pl.BlockSpec((1, l, d), lambda i, j: (i, 0, 0)), vec, vec, vec,
                  pl.BlockSpec((d, FFN_TC), lambda i, j: (0, j)),
                  pl.BlockSpec((d, FFN_TC), lambda i, j: (0, j + nj)),
                  pl.BlockSpec((9, FFN_TC), lambda i, j: (0, j)),
                  pl.BlockSpec((1, FFN_TC), lambda i, j: (0, j)),
                  pl.BlockSpec((FFN_TC, d), lambda i, j: (j, 0)),
                  pl.BlockSpec((1, d), lambda i, j: (0, 0))],
        out_specs=pl.BlockSpec((1, l, d), lambda i, j: (i, 0, 0)),
        out_shape=jax.ShapeDtypeStruct((b, l, d), F32),
        scratch_shapes=([pltpu.VMEM((l, d), BF16)] + [pltpu.VMEM((l + 2 * pad, FFN_TC), F32)] * 3
                        + [pltpu.VMEM((l, FFN_TC), F32), pltpu.VMEM((l, FFN_TC), BF16),
                           pltpu.VMEM((FFN_TC, d), BF16)]),
        compiler_params=_cparams("parallel", "arbitrary"),
        name="conv_ffn",
    )(x, gs, sh, ga, w_up, w_up, dw.reshape(9, D_FF), db.reshape(1, D_FF), w_down, g_final.reshape(1, d))


def _rope_tables(n_pos):
    half = RET_HEAD_DIM // 2
    inv_freq = ROPE_BASE ** (-jnp.arange(half, dtype=F32) / half)
    ang = jnp.arange(n_pos, dtype=jnp.int32).astype(F32)[:, None] * inv_freq[None, :]
    return jnp.tile(jnp.cos(ang), (1, RET_HEADS)), jnp.tile(jnp.sin(ang), (1, RET_HEADS))


def _split_half_perm():
    half = RET_HEAD_DIM // 2
    t, h, i = np.meshgrid(np.arange(2), np.arange(RET_HEADS), np.arange(half), indexing="ij")
    return (h * RET_HEAD_DIM + t * half + i).reshape(-1)


def kernel(x, c, ctx, c_ctx, w_ada, b_ada, g_norm1, g_norm2, w_in, b_gate, ret_decay, ret_gn, w_ret_o, conv_dw, conv_db, conv_ln_g, conv_ln_b, w_conv_o, gmlp_ln_g, gmlp_ln_b, gmlp_ws, gmlp_bs, w_gmlp_o, w_fnet_o, w_out, w_ffn_up, ffn_dw, ffn_db, w_ffn_down, g_final):
    bsz, seq, d = x.shape
    n_ctx = ctx.shape[1]
    cc = jnp.concatenate([c, c_ctx[None, :], jnp.zeros((7, d), F32)], axis=0)
    mods = _ada(cc, w_ada, b_ada)
    cos_t, sin_t = _rope_tables(n_ctx + seq)
    perm = _split_half_perm()
    zero_state = jnp.zeros((bsz, BW, BW), F32)
    xc = ctx

    def mixers(stream, l_idx, gs, sh, ga, cos_p, sin_p, sf0, sb0, tabs, want_out):
        w = w_in[l_idx]
        w_small = jnp.concatenate([w[:, :BW][:, perm], w[:, BW:2 * BW][:, perm], w[:, 2 * BW:SMALL_COLS]],
                                  axis=1).astype(BF16)
        qkvg, conf_h, gz, ff = _inproj(stream, gs, sh, w_small, cos_p, sin_p)
        if not want_out:
            return (None,) + tuple(_final_states(qkvg, tabs))
        ret_pre, conf_pre, gm_pre, sf, sb = _mixers(
            qkvg, conf_h, gz, sf0, sb0, tabs, ret_gn[l_idx].reshape(1, BW),
            _conv_consts(conv_dw[l_idx], conv_db[l_idx], conv_ln_g[l_idx], conv_ln_b[l_idx]),
            _gmlp_consts(gmlp_ln_g[l_idx], gmlp_ln_b[l_idx], gmlp_ws[l_idx], gmlp_bs[l_idx]))
        fn_pre = _fnet(ff)
        w_bo = jnp.stack([w_ret_o[l_idx], w_conv_o[l_idx], w_gmlp_o[l_idx], w_fnet_o[l_idx]]).astype(BF16)
        new = _merge(stream, gs, sh, ga, (ret_pre, conf_pre, gm_pre, fn_pre),
                     w[:, GATE_OFF:].astype(BF16), b_gate[l_idx], w_bo, w_out[l_idx].astype(BF16))
        return new, sf, sb

    for l_idx in range(DEPTH):
        last = l_idx == DEPTH - 1
        lat = [m[:, None, :] for m in jnp.split(mods[l_idx, :bsz], 6, axis=-1)]
        cm = [jnp.broadcast_to(m[None, None, :], (bsz, 1, d)) for m in jnp.split(mods[l_idx, bsz], 6, axis=-1)]
        g1 = g_norm1[l_idx][None, None, :]
        g2 = g_norm2[l_idx][None, None, :]
        tabs = _retention_tables(jax.nn.log_sigmoid(ret_decay[l_idx].astype(F32)))
        w_up = w_ffn_up[l_idx].astype(BF16)
        w_down = w_ffn_down[l_idx].astype(BF16)

        xc_mix, s_f, s_b = mixers(xc, l_idx, g1 * (1 + cm[1]), cm[0], cm[2], cos_t[:n_ctx], sin_t[:n_ctx],
                                  zero_state, zero_state, tabs, not last)
        x, _, _ = mixers(x, l_idx, g1 * (1 + lat[1]), lat[0], lat[2], cos_t[n_ctx:], sin_t[n_ctx:],
                         s_f, s_b, tabs, True)
        x = _conv_ffn(x, g2 * (1 + lat[4]), lat[3], lat[5], w_up, ffn_dw[l_idx], ffn_db[l_idx], w_down,
                      g_final, GRID_W, True, last)
        if not last:
            fold = max(1, seq // n_ctx)
            while bsz % fold:
                fold //= 2
            xc = _conv_ffn(xc_mix.reshape(bsz // fold, fold * n_ctx, d), (g2 * (1 + cm[4]))[:bsz // fold],
                           cm[3][:bsz // fold], cm[5][:bsz // fold], w_up, ffn_dw[l_idx], ffn_db[l_idx],
                           w_down, g_final, n_ctx, False, False).reshape(bsz, n_ctx, d)
    return x
```

```python
import functools

import jax
import jax.numpy as jnp
import numpy as np
from jax import lax
from jax.experimental import pallas as pl
from jax.experimental.pallas import tpu as pltpu

D_MODEL = 1024
DEPTH = 2
GRID_W = 64
RET_HEADS = 4
RET_HEAD_DIM = 64
BW = 256
CHUNK = 128
LANES = 128
ROPE_BASE = 10000.0
CONV_KERNEL = 31
GROUPS = 4
GROUP_C = BW // GROUPS
D_FF = ((8 * D_MODEL // 3 + 127) // 128) * 128
EPS = 1e-6
SMALL_COLS = 9 * BW
GATE_OFF = SMALL_COLS

F32 = jnp.float32
BF16 = jnp.bfloat16
HIGHEST = lax.Precision.HIGHEST
VMEM_LIMIT_BYTES = 56 * 1024 * 1024


def _cparams(*sem):
    return pltpu.CompilerParams(dimension_semantics=sem, vmem_limit_bytes=VMEM_LIMIT_BYTES)


def _dot(a, b):
    return jnp.dot(a, b, preferred_element_type=F32)


def _modnorm(x, gs, sh):
    ms = jnp.mean(x * x, axis=-1, keepdims=True)
    return x * lax.rsqrt(ms + EPS) * gs + sh


def _sigmoid(x):
    return 1.0 / (1.0 + jnp.exp(-x))


def _silu(x):
    return x * _sigmoid(x)


def _layer_norm(x, g, b):
    xc = x - jnp.mean(x, axis=-1, keepdims=True)
    return xc * lax.rsqrt(jnp.mean(xc * xc, axis=-1, keepdims=True) + EPS) * g + b


def _ada_kernel(c_ref, w_ref, b_ref, o_ref):
    o_ref[0] = jnp.dot(_silu(c_ref[...]), w_ref[0], precision=HIGHEST,
                       preferred_element_type=F32) + b_ref[0]


def _ada(cc, w_ada, b_ada):
    n, d = cc.shape
    cols = w_ada.shape[-1]
    tn = 1536
    return pl.pallas_call(
        _ada_kernel,
        grid=(DEPTH, cols // tn),
        in_specs=[pl.BlockSpec((n, d), lambda l, j: (0, 0)),
                  pl.BlockSpec((1, d, tn), lambda l, j: (l, 0, j)),
                  pl.BlockSpec((1, 1, tn), lambda l, j: (l, 0, j))],
        out_specs=pl.BlockSpec((1, n, tn), lambda l, j: (l, 0, j)),
        out_shape=jax.ShapeDtypeStruct((DEPTH, n, cols), F32),
        compiler_params=_cparams("parallel", "parallel"),
        name="ada",
    )(cc, w_ada, b_ada.reshape(DEPTH, 1, cols))


def _inproj_kernel(x_ref, gs_ref, sh_ref, w_ref, cos_ref, sin_ref,
                   qkvg_ref, conf_ref, gz_ref, ff_ref, hx_ref, pf_ref):
    hx_ref[...] = _modnorm(x_ref[0], gs_ref[0], sh_ref[0]).astype(BF16)
    cos = cos_ref[...]
    sin = sin_ref[...]
    half = BW // 2

    def rot(p, scale):
        p1, p2 = p[:, :half], p[:, half:]
        return jnp.concatenate([(p1 * cos - p2 * sin) * scale, (p1 * sin + p2 * cos) * scale], axis=-1)

    pq = _dot(hx_ref[...], w_ref[:, 0:BW])
    qkvg_ref[0, :, 0:BW] = rot(pq, 1.0).astype(BF16)
    pk = _dot(hx_ref[...], w_ref[:, BW:2 * BW])
    qkvg_ref[0, :, BW:2 * BW] = rot(pk, RET_HEAD_DIM ** -0.5).astype(BF16)
    pv = _dot(hx_ref[...], w_ref[:, 2 * BW:4 * BW])
    qkvg_ref[0, :, 2 * BW:4 * BW] = pv.astype(BF16)
    pc = _dot(hx_ref[...], w_ref[:, 4 * BW:6 * BW])
    conf_ref[0] = (pc[:, :BW] * _sigmoid(pc[:, BW:])).astype(BF16)
    gz_ref[0] = _dot(hx_ref[...], w_ref[:, 6 * BW:8 * BW]).astype(BF16)
    pf = _dot(hx_ref[...], w_ref[:, 8 * BW:9 * BW])
    rows = pf.shape[0] // FNET_RADIX
    for c in range(BW // LANES):
        pf_ref[c] = pf[:, c * LANES:(c + 1) * LANES]
        for n1 in range(FNET_RADIX):
            ff_ref[0, :, n1 * BW + c * LANES:n1 * BW + (c + 1) * LANES] = (
                pf_ref[c, pl.ds(n1, rows, stride=FNET_RADIX), :].astype(BF16))


def _inproj(x, gs, sh, w_small, cos_t, sin_t):
    b, l, d = x.shape
    tm = min(l, 512)
    vec = pl.BlockSpec((1, 1, d), lambda i, j: (i, 0, 0))
    tab = pl.BlockSpec((tm, BW // 2), lambda i, j: (j, 0))

    def out(c):
        return pl.BlockSpec((1, tm, c), lambda i, j: (i, j, 0))

    return pl.pallas_call(
        _inproj_kernel,
        grid=(b, l // tm),
        in_specs=[pl.BlockSpec((1, tm, d), lambda i, j: (i, j, 0)), vec, vec,
                  pl.BlockSpec((d, SMALL_COLS), lambda i, j: (0, 0)), tab, tab],
        out_specs=[out(4 * BW), out(BW), out(2 * BW),
                   pl.BlockSpec((1, tm // FNET_RADIX, FNET_RADIX * BW), lambda i, j: (i, j, 0))],
        out_shape=[jax.ShapeDtypeStruct((b, l, c), BF16) for c in (4 * BW, BW, 2 * BW)]
        + [jax.ShapeDtypeStruct((b, l // FNET_RADIX, FNET_RADIX * BW), BF16)],
        scratch_shapes=[pltpu.VMEM((tm, d), BF16), pltpu.VMEM((BW // LANES, tm, LANES), F32)],
        compiler_params=_cparams("parallel", "parallel"),
        name="inproj",
    )(x, gs, sh, w_small, cos_t, sin_t)


def _mixer_kernel(n_chunks, qkvg_ref, conf_ref, gz_ref, sf0_ref, sb0_ref,
                  mst_ref, wqf_ref, wqb_ref, wkf_ref, wkb_ref, cdf_ref, cdb_ref, bd_ref, avg_ref, gn_ref, hm_ref,
                  cw_ref, cb_ref, cg_ref, cbe_ref, mg_ref, mbe_ref, mws_ref, mbs_ref, mgm_ref,
                  out_ref, cout_ref, mout_ref, sf_ref, sb_ref, o_scr, uf_scr, ub_scr, st_scr, hp_scr):
    unroll = 4 if n_chunks % 4 == 0 else (2 if n_chunks % 2 == 0 else 1)
    l = n_chunks * CHUNK
    hp_scr[0:CONV_PAD, :] = jnp.zeros((CONV_PAD, BW), F32)
    hp_scr[CONV_PAD + l:2 * CONV_PAD + l, :] = jnp.zeros((CONV_PAD, BW), F32)

    def chunk_rows(n):
        return pl.ds(pl.multiple_of(n * CHUNK, CHUNK), CHUNK)

    def state_update(k, wk_ref, v):
        kdec = (k * wk_ref[...]).astype(BF16)
        upd = lax.dot_general(kdec, v, (((0,), (0,)), ((), ())), preferred_element_type=F32)
        return bd_ref[...] * upd

    def intra(n, carry):
        rows = chunk_rows(n)
        q = qkvg_ref[0, rows, 0:BW]
        k = qkvg_ref[0, rows, BW:2 * BW]
        v = qkvg_ref[0, rows, 2 * BW:3 * BW]
        kf = k.astype(F32)
        qst = jnp.concatenate([q * hm_ref[h] for h in range(RET_HEADS)], axis=0)
        s = lax.dot_general(qst, k, (((1,), (1,)), ((), ())), preferred_element_type=F32)
        p = (s * mst_ref[...]).astype(BF16)
        pcat = jnp.concatenate([p[h * CHUNK:(h + 1) * CHUNK] for h in range(RET_HEADS)], axis=1)
        vbd = jnp.concatenate([v * hm_ref[RET_HEADS + h] for h in range(RET_HEADS)], axis=0)
        o_scr[rows, :] = _dot(pcat, vbd)
        uf_scr[n] = state_update(kf, wkf_ref, v)
        ub_scr[n] = state_update(kf, wkb_ref, v)
        mout_ref[0, rows, :] = _gmlp_chunk(gz_ref[0, rows, :], mg_ref, mbe_ref, mws_ref, mbs_ref,
                                           mgm_ref).astype(BF16)
        hp_scr[pl.ds(pl.multiple_of(n * CHUNK + CONV_PAD, 8), CHUNK), :] = conf_ref[0, rows, :].astype(F32)
        return carry

    lax.fori_loop(0, n_chunks, intra, 0, unroll=unroll)

    sf_ref[0] = sf0_ref[0]
    sb_ref[0] = sb0_ref[0]

    def scan(i, carry):
        st_scr[i, 0:BW, :] = sf_ref[0].astype(BF16)
        sf_ref[0] = cdf_ref[...] * sf_ref[0] + uf_scr[i]
        n = n_chunks - 1 - i
        st_scr[n, BW:2 * BW, :] = sb_ref[0].astype(BF16)
        sb_ref[0] = cdb_ref[...] * sb_ref[0] + ub_scr[n]
        return carry

    lax.fori_loop(0, n_chunks, scan, 0)

    def group_mean(t):
        hi = t.astype(BF16)
        lo = (t - hi.astype(F32)).astype(BF16)
        return _dot(jnp.concatenate([hi, lo], axis=1), avg_ref[...])

    def finish(n, carry):
        rows = chunk_rows(n)
        q = qkvg_ref[0, rows, 0:BW].astype(F32)
        qdec = jnp.concatenate([(q * wqf_ref[...]).astype(BF16), (q * wqb_ref[...]).astype(BF16)], axis=1)
        o = o_scr[rows, :] + _dot(qdec, st_scr[n])
        oc = o - group_mean(o)
        y = oc * lax.rsqrt(group_mean(oc * oc) + EPS) * gn_ref[...]
        g = qkvg_ref[0, rows, 3 * BW:4 * BW].astype(F32)
        out_ref[0, rows, :] = (_silu(g) * y).astype(BF16)
        for t in range(CHUNK // CONV_TILE):
            base = pl.multiple_of(n * CHUNK + t * CONV_TILE, CONV_TILE)
            cout_ref[0, pl.ds(base, CONV_TILE), :] = _conv_tile(hp_scr, base, cw_ref, cb_ref, cg_ref,
                                                                cbe_ref).astype(BF16)
        return carry

    lax.fori_loop(0, n_chunks, finish, 0, unroll=unroll)


def _mixers(qkvg, conf_h, gz, sf0, sb0, tabs, gn, conv_consts, gmlp_consts):
    b, l, _ = qkvg.shape
    n_chunks = l // CHUNK
    st = pl.BlockSpec((1, BW, BW), lambda i: (i, 0, 0))

    def seq(c):
        return pl.BlockSpec((1, l, c), lambda i: (i, 0, 0))

    def full(a):
        return pl.BlockSpec(a.shape, lambda i: (0,) * a.ndim)

    consts = ([tabs[k] for k in ("mst", "wqf", "wqb", "wkf", "wkb", "cdf", "cdb", "bd", "avg")] + [gn, tabs["hm"]]
              + conv_consts + gmlp_consts)
    return pl.pallas_call(
        functools.partial(_mixer_kernel, n_chunks),
        grid=(b,),
        in_specs=[seq(4 * BW), seq(BW), seq(2 * BW), st, st] + [full(a) for a in consts],
        out_specs=[seq(BW), seq(BW), seq(BW), st, st],
        out_shape=[jax.ShapeDtypeStruct((b, l, BW), BF16)] * 3 + [jax.ShapeDtypeStruct((b, BW, BW), F32)] * 2,
        scratch_shapes=[pltpu.VMEM((l, BW), F32),
                        pltpu.VMEM((n_chunks, BW, BW), F32),
                        pltpu.VMEM((n_chunks, BW, BW), F32),
                        pltpu.VMEM((n_chunks, 2 * BW, BW), BF16),
                        pltpu.VMEM((l + 2 * CONV_PAD, BW), F32)],
        compiler_params=_cparams("parallel"),
        name="mixers",
    )(qkvg, conf_h, gz, sf0, sb0, *consts)


def _state_kernel(n_chunks, k_ref, v_ref, wkf_ref, wkb_ref, cdf_ref, cdb_ref, bd_ref, sf_ref, sb_ref):
    sf_ref[0] = jnp.zeros((BW, BW), F32)
    sb_ref[0] = jnp.zeros((BW, BW), F32)

    def increment(n, wk_ref):
        rows = pl.ds(pl.multiple_of(n * CHUNK, CHUNK), CHUNK)
        kdec = (k_ref[0, rows, :].astype(F32) * wk_ref[...]).astype(BF16)
        upd = lax.dot_general(kdec, v_ref[0, rows, :], (((0,), (0,)), ((), ())), preferred_element_type=F32)
        return bd_ref[...] * upd

    def scan(i, carry):
        sf_ref[0] = cdf_ref[...] * sf_ref[0] + increment(i, wkf_ref)
        sb_ref[0] = cdb_ref[...] * sb_ref[0] + increment(n_chunks - 1 - i, wkb_ref)
        return carry

    lax.fori_loop(0, n_chunks, scan, 0)


def _final_states(qkvg, tabs):
    b, l, _ = qkvg.shape
    st = pl.BlockSpec((1, BW, BW), lambda i: (i, 0, 0))

    def full(a):
        return pl.BlockSpec(a.shape, lambda i: (0,) * a.ndim)

    consts = [tabs[k] for k in ("wkf", "wkb", "cdf", "cdb", "bd")]
    return pl.pallas_call(
        functools.partial(_state_kernel, l // CHUNK),
        grid=(b,),
        in_specs=[pl.BlockSpec((1, l, BW), lambda i: (i, 0, 1)),
                  pl.BlockSpec((1, l, BW), lambda i: (i, 0, 2))] + [full(a) for a in consts],
        out_specs=[st, st],
        out_shape=[jax.ShapeDtypeStruct((b, BW, BW), F32)] * 2,
        compiler_params=_cparams("parallel"),
        name="final_states",
    )(qkvg, qkvg, *consts)


def _retention_tables(log_g):
    lf, lb = log_g[0], log_g[1]
    idx = jnp.arange(CHUNK, dtype=F32)
    diff = idx[:, None] - idx[None, :]
    m = (jnp.where(diff > 0, jnp.exp(lf[:, None, None] * jnp.maximum(diff, 0.0)), 0.0)
         + jnp.where(diff < 0, jnp.exp(lb[:, None, None] * jnp.maximum(-diff, 0.0)), 0.0)
         + jnp.where(diff == 0, 2.0, 0.0))
    lane = np.arange(BW)
    hq = (lane % (BW // 2)) // (RET_HEAD_DIM // 2)
    hv = lane // RET_HEAD_DIM
    lfq, lbq = lf[hq][None, :], lb[hq][None, :]
    i = idx[:, None]
    hm = np.concatenate([(hq[None, :] == np.arange(RET_HEADS)[:, None]),
                         (hv[None, :] == np.arange(RET_HEADS)[:, None])], axis=0).astype(np.float32)
    return dict(
        mst=m.reshape(RET_HEADS * CHUNK, CHUNK),
        wqf=jnp.exp(lfq * (i + 1.0)), wqb=jnp.exp(lbq * (CHUNK - i)),
        wkf=jnp.exp(lfq * (CHUNK - 1.0 - i)), wkb=jnp.exp(lbq * i),
        cdf=jnp.broadcast_to(jnp.exp(lf[hq] * CHUNK)[:, None], (BW, BW)),
        cdb=jnp.broadcast_to(jnp.exp(lb[hq] * CHUNK)[:, None], (BW, BW)),
        bd=jnp.asarray((hq[:, None] == hv[None, :]).astype(np.float32)),
        avg=jnp.asarray(np.tile((hv[:, None] == hv[None, :]).astype(np.float32) / RET_HEAD_DIM,
                                (2, 1))).astype(BF16),
        hm=jnp.asarray(np.broadcast_to(hm[:, None, :], (2 * RET_HEADS, CHUNK, BW))).astype(BF16),
    )


CONV_TILE = 128
CONV_PAD = 16


def _conv_tile(hp_ref, base, w_ref, b_ref, g_ref, be_ref):
    first = CONV_PAD - (CONV_KERNEL - 1) // 2
    win = hp_ref[pl.ds(base, CONV_TILE + 2 * CONV_PAD), :]
    acc = jnp.zeros((CONV_TILE, BW), F32) + b_ref[...]
    for s in range(8):
        part = None
        for m in range(4):
            k = 8 * m + s - first
            if 0 <= k < CONV_KERNEL:
                term = win[8 * m:8 * m + CONV_TILE + 8, :] * w_ref[k:k + 1, :]
                part = term if part is None else part + term
        if part is not None:
            acc = acc + part[s:s + CONV_TILE, :]
    return _silu(_layer_norm(acc, g_ref[...], be_ref[...]))


def _conv_consts(w_dw, b_dw, ln_g, ln_b):
    return [w_dw, b_dw.reshape(1, BW), ln_g.reshape(1, BW), ln_b.reshape(1, BW)]


def _gmlp_chunk(z, g_ref, be_ref, ws_ref, bs_ref, gm_ref):
    z = z.astype(F32)
    z = 0.5 * z * (1.0 + lax.erf(z * (2.0 ** -0.5)))
    u, v = z[:, :BW], z[:, BW:]
    v = _layer_norm(v, g_ref[...], be_ref[...]).astype(BF16)
    full = _dot(ws_ref[...], v)
    sv = bs_ref[...]
    for g in range(GROUPS):
        sv = sv + full[g * CHUNK:(g + 1) * CHUNK, :] * gm_ref[g:g + 1, :]
    return u * sv


def _gmlp_consts(ln_g, ln_b, ws, bs):
    gm = np.repeat(np.eye(GROUPS, dtype=np.float32), GROUP_C, axis=1)
    return [ln_g.reshape(1, BW), ln_b.reshape(1, BW), ws.reshape(GROUPS * CHUNK, CHUNK).astype(BF16),
            jnp.repeat(bs.T, GROUP_C, axis=1), jnp.asarray(gm)]


FNET_RADIX = 4


@functools.lru_cache(maxsize=None)
def _dft_tables(l):
    n2 = l // FNET_RADIX
    c = np.arange(GROUP_C, dtype=np.int64)
    angc = 2.0 * np.pi * ((c[:, None] * c[None, :]) % GROUP_C) / GROUP_C
    eye = np.eye(GROUPS)
    wcs = np.concatenate([np.kron(eye, np.cos(angc)), np.kron(eye, np.sin(angc))], axis=1)
    n = np.arange(n2, dtype=np.int64)
    ang = 2.0 * np.pi * ((n[:, None] * n[None, :]) % n2) / n2
    cc, ss = np.cos(ang), np.sin(ang)
    m1 = np.block([[cc, ss], [-ss, cc]])
    angt = 2.0 * np.pi * (n[:, None] * np.arange(FNET_RADIX)[None, :]) / l
    tc = np.repeat(np.cos(angt), BW, axis=1)
    ts = np.repeat(np.sin(angt), BW, axis=1)
    return tuple(t.astype(np.float32) for t in (wcs, m1, tc, ts))


def _fnet_kernel(n2, scale, f_ref, wcs_ref, m1_ref, tc_ref, ts_ref, out_ref, ab_ref, zr_ref, zi_ref):
    for i in range(FNET_RADIX):
        cols = slice(i * BW, (i + 1) * BW)
        cs = _dot(f_ref[0, :, cols], wcs_ref[...])
        ab_ref[0:n2, cols] = cs[:, :BW].astype(BF16)
        ab_ref[n2:2 * n2, cols] = (-cs[:, BW:]).astype(BF16)
    for i in range(FNET_RADIX):
        cols = slice(i * BW, (i + 1) * BW)
        y = _dot(m1_ref[...], ab_ref[:, cols])
        yr, yi = y[:n2], y[n2:]
        if i == 0:
            zr_ref[:, cols] = yr
            zi_ref[:, cols] = yi
        else:
            tc, ts = tc_ref[:, cols], ts_ref[:, cols]
            zr_ref[:, cols] = yr * tc + yi * ts
            zi_ref[:, cols] = yi * tc - yr * ts
    step = min(n2, 128)
    for t in range(n2 // step):
        rows = slice(t * step, (t + 1) * step)
        zr = [zr_ref[rows, i * BW:(i + 1) * BW] for i in range(FNET_RADIX)]
        zi = [zi_ref[rows, i * BW:(i + 1) * BW] for i in range(FNET_RADIX)]
        even, odd = zr[0] + zr[2], zr[1] + zr[3]
        deven, dodd = zr[0] - zr[2], zi[1] - zi[3]
        for k1, val in enumerate((even + odd, deven + dodd, even - odd, deven - dodd)):
            out_ref[0, k1 * n2 + t * step:k1 * n2 + (t + 1) * step, :] = (val * scale).astype(BF16)


def _fnet(f):
    b, n2, _ = f.shape
    l = n2 * FNET_RADIX
    wcs, m1, tc, ts = _dft_tables(l)
    scale = float(1.0 / np.sqrt(l * GROUP_C))

    def full(a):
        return pl.BlockSpec(a.shape, lambda i: (0,) * a.ndim)

    consts = [jnp.asarray(wcs).astype(BF16), jnp.asarray(m1).astype(BF16), jnp.asarray(tc), jnp.asarray(ts)]
    return pl.pallas_call(
        functools.partial(_fnet_kernel, n2, scale),
        grid=(b,),
        in_specs=[pl.BlockSpec((1, n2, FNET_RADIX * BW), lambda i: (i, 0, 0))] + [full(a) for a in consts],
        out_specs=pl.BlockSpec((1, l, BW), lambda i: (i, 0, 0)),
        out_shape=jax.ShapeDtypeStruct((b, l, BW), BF16),
        scratch_shapes=[pltpu.VMEM((2 * n2, FNET_RADIX * BW), BF16),
                        pltpu.VMEM((n2, FNET_RADIX * BW), F32),
                        pltpu.VMEM((n2, FNET_RADIX * BW), F32)],
        compiler_params=_cparams("parallel"),
        name="fnet",
    )(f, *consts)


def _merge_kernel(x_ref, gs_ref, sh_ref, ga_ref, p0_ref, p1_ref, p2_ref, p3_ref,
                  wg_ref, bg_ref, wo_ref, wout_ref, out_ref, hx_ref, m_ref):
    x = x_ref[0]
    hx_ref[...] = _modnorm(x, gs_ref[0], sh_ref[0]).astype(BF16)
    for i, p_ref in enumerate((p0_ref, p1_ref, p2_ref, p3_ref)):
        cols = slice(i * D_MODEL, (i + 1) * D_MODEL)
        gate = _sigmoid(_dot(hx_ref[...], wg_ref[:, cols]) + bg_ref[:, cols])
        term = gate * _dot(p_ref[0], wo_ref[i])
        if i == 0:
            m_ref[...] = term
        else:
            m_ref[...] += term
    out_ref[0] = x + ga_ref[0] * _dot(m_ref[...].astype(BF16), wout_ref[...])


def _merge(x, gs, sh, ga, pres, w_gate, b_gate, w_bo, w_out):
    b, l, d = x.shape
    tm = min(l, 512)
    vec = pl.BlockSpec((1, 1, d), lambda i, j: (i, 0, 0))
    pre = pl.BlockSpec((1, tm, BW), lambda i, j: (i, j, 0))
    return pl.pallas_call(
        _merge_kernel,
        grid=(b, l // tm),
        in_specs=[pl.BlockSpec((1, tm, d), lambda i, j: (i, j, 0)), vec, vec, vec, pre, pre, pre, pre,
                  pl.BlockSpec((d, 4 * d), lambda i, j: (0, 0)),
                  pl.BlockSpec((1, 4 * d), lambda i, j: (0, 0)),
                  pl.BlockSpec((4, BW, d), lambda i, j: (0, 0, 0)),
                  pl.BlockSpec((d, d), lambda i, j: (0, 0))],
        out_specs=pl.BlockSpec((1, tm, d), lambda i, j: (i, j, 0)),
        out_shape=jax.ShapeDtypeStruct((b, l, d), F32),
        scratch_shapes=[pltpu.VMEM((tm, d), BF16), pltpu.VMEM((tm, d), F32)],
        compiler_params=_cparams("parallel", "parallel"),
        name="merge",
    )(x, gs, sh, ga, *pres, w_gate, b_gate.reshape(1, 4 * d), w_bo, w_out)


FFN_TC = 256
FFN_ROWS = 256
FFN_SUB = 128


def _ffn_pad(grid_w, row_conv):
    return (grid_w if row_conv else 0) + 8


def _ffn_kernel(l, grid_w, row_conv, final_norm, x_ref, gs_ref, sh_ref, ga_ref, wa_ref, wb_ref, dw_ref,
                db_ref, wd_ref, gf_ref, out_ref, hx_ref, ac_ref, al_ref, ar_ref, b_ref, h_ref, wds_ref):
    j = pl.program_id(1)
    pad = _ffn_pad(grid_w, row_conv)
    n_tiles = l // FFN_ROWS

    @pl.when(j == 0)
    def _():
        for t in range(n_tiles):
            rows = slice(t * FFN_ROWS, (t + 1) * FFN_ROWS)
            x = x_ref[0, rows, :]
            hx_ref[rows, :] = _modnorm(x, gs_ref[0], sh_ref[0]).astype(BF16)
            out_ref[0, rows, :] = x
        for ref in (ac_ref, al_ref, ar_ref):
            ref[0:pad, :] = jnp.zeros((pad, FFN_TC), F32)
            ref[pad + l:2 * pad + l, :] = jnp.zeros((pad, FFN_TC), F32)

    wds_ref[...] = (wd_ref[...].astype(F32) * ga_ref[0]).astype(BF16)

    col = lax.broadcasted_iota(jnp.int32, (FFN_ROWS, FFN_TC), 0) % grid_w
    not_first = col != 0
    not_last = col != grid_w - 1

    step = min(l, 512)
    for t in range(l // step):
        rows = slice(t * step, (t + 1) * step)
        ac_ref[pad + t * step:pad + (t + 1) * step, :] = _dot(hx_ref[rows, :], wa_ref[...])
        b_ref[rows, :] = _dot(hx_ref[rows, :], wb_ref[...])

    for t in range(n_tiles):
        lo = pad + t * FFN_ROWS
        al_ref[lo:lo + FFN_ROWS, :] = jnp.where(not_first, ac_ref[lo - 1:lo - 1 + FFN_ROWS, :], 0.0)
        ar_ref[lo:lo + FFN_ROWS, :] = jnp.where(not_last, ac_ref[lo + 1:lo + 1 + FFN_ROWS, :], 0.0)

    for t in range(l // FFN_SUB):
        acc = jnp.zeros((FFN_SUB, FFN_TC), F32) + db_ref[...]
        for dr in ((-1, 0, 1) if row_conv else (0,)):
            off = pad + t * FFN_SUB + dr * grid_w
            tap = 3 * (dr + 1)
            acc = acc + al_ref[off:off + FFN_SUB, :] * dw_ref[tap:tap + 1, :]
            acc = acc + ac_ref[off:off + FFN_SUB, :] * dw_ref[tap + 1:tap + 2, :]
            acc = acc + ar_ref[off:off + FFN_SUB, :] * dw_ref[tap + 2:tap + 3, :]
        rows = slice(t * FFN_SUB, (t + 1) * FFN_SUB)
        h_ref[rows, :] = (_silu(acc) * b_ref[rows, :]).astype(BF16)

    for t in range(l // step):
        rows = slice(t * step, (t + 1) * step)
        out_ref[0, rows, :] += _dot(h_ref[rows, :], wds_ref[...])

    if final_norm:
        @pl.when(j == pl.num_programs(1) - 1)
        def _():
            for t in range(l // FFN_ROWS):
                rows = slice(t * FFN_ROWS, (t + 1) * FFN_ROWS)
                y = out_ref[0, rows, :]
                out_ref[0, rows, :] = y * lax.rsqrt(jnp.mean(y * y, axis=-1, keepdims=True) + EPS) * gf_ref[...]


def _conv_ffn(x, gs, sh, ga, w_up, dw, db, w_down, g_final, grid_w, row_conv, final_norm):
    b, l, d = x.shape
    nj = D_FF // FFN_TC
    pad = _ffn_pad(grid_w, row_conv)
    vec = pl.BlockSpec((1, 1, d), lambda i, j: (i, 0, 0))
    return pl.pallas_call(
        functools.partial(_ffn_kernel, l, grid_w, row_conv, final_norm),
        grid=(b, nj),
        in_specs=[pl.BlockSpec((1, l, d), lambda i, j: (i, 0, 0)), vec, vec, vec,
                  pl.BlockSpec((d, FFN_TC), lambda i, j: (0, j)),
                  pl.BlockSpec((d, FFN_TC), lambda i, j: (0, j + nj)),
                  pl.BlockSpec((9, FFN_TC), lambda i, j: (0, j)),
                  pl.BlockSpec((1, FFN_TC), lambda i, j: (0, j)),
                  pl.BlockSpec((FFN_TC, d), lambda i, j: (j, 0)),
                  pl.BlockSpec((1, d), lambda i, j: (0, 0))],
        out_specs=pl.BlockSpec((1, l, d), lambda i, j: (i, 0, 0)),
        out_shape=jax.ShapeDtypeStruct((b, l, d), F32),
        scratch_shapes=([pltpu.VMEM((l, d), BF16)] + [pltpu.VMEM((l + 2 * pad, FFN_TC), F32)] * 3
                        + [pltpu.VMEM((l, FFN_TC), F32), pltpu.VMEM((l, FFN_TC), BF16),
                           pltpu.VMEM((FFN_TC, d), BF16)]),
        compiler_params=_cparams("parallel", "arbitrary"),
        name="conv_ffn",
    )(x, gs, sh, ga, w_up, w_up, dw.reshape(9, D_FF), db.reshape(1, D_FF), w_down, g_final.reshape(1, d))


def _rope_tables(n_pos):
    half = RET_HEAD_DIM // 2
    inv_freq = ROPE_BASE ** (-jnp.arange(half, dtype=F32) / half)
    ang = jnp.arange(n_pos, dtype=jnp.int32).astype(F32)[:, None] * inv_freq[None, :]
    return jnp.tile(jnp.cos(ang), (1, RET_HEADS)), jnp.tile(jnp.sin(ang), (1, RET_HEADS))


def _split_half_perm():
    half = RET_HEAD_DIM // 2
    t, h, i = np.meshgrid(np.arange(2), np.arange(RET_HEADS), np.arange(half), indexing="ij")
    return (h * RET_HEAD_DIM + t * half + i).reshape(-1)


def kernel(x, c, ctx, c_ctx, w_ada, b_ada, g_norm1, g_norm2, w_in, b_gate, ret_decay, ret_gn, w_ret_o, conv_dw, conv_db, conv_ln_g, conv_ln_b, w_conv_o, gmlp_ln_g, gmlp_ln_b, gmlp_ws, gmlp_bs, w_gmlp_o, w_fnet_o, w_out, w_ffn_up, ffn_dw, ffn_db, w_ffn_down, g_final):
    bsz, seq, d = x.shape
    n_ctx = ctx.shape[1]
    cc = jnp.concatenate([c, c_ctx[None, :], jnp.zeros((7, d), F32)], axis=0)
    mods = _ada(cc, w_ada, b_ada)
    cos_t, sin_t = _rope_tables(n_ctx + seq)
    perm = _split_half_perm()
    zero_state = jnp.zeros((bsz, BW, BW), F32)
    xc = ctx

    def mixers(stream, l_idx, gs, sh, ga, cos_p, sin_p, sf0, sb0, tabs, want_out):
        w_small = jnp.concatenate([w_in[l_idx, :, :BW][:, perm], w_in[l_idx, :, BW:2 * BW][:, perm],
                                   w_in[l_idx, :, 2 * BW:SMALL_COLS]], axis=1).astype(BF16)
        qkvg, conf_h, gz, ff = _inproj(stream, gs, sh, w_small, cos_p, sin_p)
        if not want_out:
            return (None,) + tuple(_final_states(qkvg, tabs))
        ret_pre, conf_pre, gm_pre, sf, sb = _mixers(
            qkvg, conf_h, gz, sf0, sb0, tabs, ret_gn[l_idx].reshape(1, BW),
            _conv_consts(conv_dw[l_idx], conv_db[l_idx], conv_ln_g[l_idx], conv_ln_b[l_idx]),
            _gmlp_consts(gmlp_ln_g[l_idx], gmlp_ln_b[l_idx], gmlp_ws[l_idx], gmlp_bs[l_idx]))
        fn_pre = _fnet(ff)
        w_bo = jnp.stack([w_ret_o[l_idx], w_conv_o[l_idx], w_gmlp_o[l_idx], w_fnet_o[l_idx]]).astype(BF16)
        new = _merge(stream, gs, sh, ga, (ret_pre, conf_pre, gm_pre, fn_pre),
                     w_in[l_idx, :, GATE_OFF:].astype(BF16), b_gate[l_idx], w_bo, w_out[l_idx].astype(BF16))
        return new, sf, sb

    for l_idx in range(DEPTH):
        last = l_idx == DEPTH - 1
        lat = [m[:, None, :] for m in jnp.split(mods[l_idx, :bsz], 6, axis=-1)]
        cm = [jnp.broadcast_to(m[None, None, :], (bsz, 1, d)) for m in jnp.split(mods[l_idx, bsz], 6, axis=-1)]
        g1 = g_norm1[l_idx][None, None, :]
        g2 = g_norm2[l_idx][None, None, :]
        tabs = _retention_tables(jax.nn.log_sigmoid(ret_decay[l_idx].astype(F32)))
        w_up = w_ffn_up[l_idx].astype(BF16)
        w_down = w_ffn_down[l_idx].astype(BF16)

        xc_mix, s_f, s_b = mixers(xc, l_idx, g1 * (1 + cm[1]), cm[0], cm[2], cos_t[:n_ctx], sin_t[:n_ctx],
                                  zero_state, zero_state, tabs, not last)
        x, _, _ = mixers(x, l_idx, g1 * (1 + lat[1]), lat[0], lat[2], cos_t[n_ctx:], sin_t[n_ctx:],
                         s_f, s_b, tabs, True)
        x = _conv_ffn(x, g2 * (1 + lat[4]), lat[3], lat[5], w_up, ffn_dw[l_idx], ffn_db[l_idx], w_down,
                      g_final, GRID_W, True, last)
        if not last:
            fold = max(1, seq // n_ctx)
            while bsz % fold:
                fold //= 2
            xc = _conv_ffn(xc_mix.reshape(bsz // fold, fold * n_ctx, d), (g2 * (1 + cm[4]))[:bsz // fold],
                           cm[3][:bsz // fold], cm[5][:bsz // fold], w_up, ffn_dw[l_idx], ffn_db[l_idx],
                           w_down, g_final, n_ctx, False, False).reshape(bsz, n_ctx, d)
    return x
```

```python
import functools

import jax
import jax.numpy as jnp
import numpy as np
from jax import lax
from jax.experimental import pallas as pl
from jax.experimental.pallas import tpu as pltpu

D_MODEL = 1024
DEPTH = 2
GRID_W = 64
RET_HEADS = 4
RET_HEAD_DIM = 64
BW = 256
CHUNK = 128
LANES = 128
ROPE_BASE = 10000.0
CONV_KERNEL = 31
GROUPS = 4
GROUP_C = BW // GROUPS
D_FF = ((8 * D_MODEL // 3 + 127) // 128) * 128
EPS = 1e-6
SMALL_COLS = 9 * BW
GATE_OFF = SMALL_COLS

F32 = jnp.float32
BF16 = jnp.bfloat16
HIGHEST = lax.Precision.HIGHEST
VMEM_LIMIT_BYTES = 56 * 1024 * 1024


def _cparams(*sem):
    return pltpu.CompilerParams(dimension_semantics=sem, vmem_limit_bytes=VMEM_LIMIT_BYTES)


def _dot(a, b):
    return jnp.dot(a, b, preferred_element_type=F32)


def _modnorm(x, gs, sh):
    ms = jnp.mean(x * x, axis=-1, keepdims=True)
    return x * lax.rsqrt(ms + EPS) * gs + sh


def _sigmoid(x):
    return 1.0 / (1.0 + jnp.exp(-x))


def _silu(x):
    return x * _sigmoid(x)


def _layer_norm(x, g, b):
    xc = x - jnp.mean(x, axis=-1, keepdims=True)
    return xc * lax.rsqrt(jnp.mean(xc * xc, axis=-1, keepdims=True) + EPS) * g + b


def _ada_kernel(c_ref, w_ref, b_ref, o_ref):
    o_ref[0] = jnp.dot(_silu(c_ref[...]), w_ref[0], precision=HIGHEST,
                       preferred_element_type=F32) + b_ref[0]


def _ada(cc, w_ada, b_ada):
    n, d = cc.shape
    cols = w_ada.shape[-1]
    tn = 1536
    return pl.pallas_call(
        _ada_kernel,
        grid=(DEPTH, cols // tn),
        in_specs=[pl.BlockSpec((n, d), lambda l, j: (0, 0)),
                  pl.BlockSpec((1, d, tn), lambda l, j: (l, 0, j)),
                  pl.BlockSpec((1, 1, tn), lambda l, j: (l, 0, j))],
        out_specs=pl.BlockSpec((1, n, tn), lambda l, j: (l, 0, j)),
        out_shape=jax.ShapeDtypeStruct((DEPTH, n, cols), F32),
        compiler_params=_cparams("parallel", "parallel"),
        name="ada",
    )(cc, w_ada, b_ada.reshape(DEPTH, 1, cols))


def _inproj_kernel(x_ref, gs_ref, sh_ref, w_ref, cos_ref, sin_ref,
                   qkvg_ref, conf_ref, gz_ref, ff_ref, hx_ref, pf_ref):
    hx_ref[...] = _modnorm(x_ref[0], gs_ref[0], sh_ref[0]).astype(BF16)
    cos = cos_ref[...]
    sin = sin_ref[...]
    half = BW // 2

    def rot(p, scale):
        p1, p2 = p[:, :half], p[:, half:]
        return jnp.concatenate([(p1 * cos - p2 * sin) * scale, (p1 * sin + p2 * cos) * scale], axis=-1)

    pq = _dot(hx_ref[...], w_ref[:, 0:BW])
    qkvg_ref[0, :, 0:BW] = rot(pq, 1.0).astype(BF16)
    pk = _dot(hx_ref[...], w_ref[:, BW:2 * BW])
    qkvg_ref[0, :, BW:2 * BW] = rot(pk, RET_HEAD_DIM ** -0.5).astype(BF16)
    pv = _dot(hx_ref[...], w_ref[:, 2 * BW:4 * BW])
    qkvg_ref[0, :, 2 * BW:4 * BW] = pv.astype(BF16)
    pc = _dot(hx_ref[...], w_ref[:, 4 * BW:6 * BW])
    conf_ref[0] = (pc[:, :BW] * _sigmoid(pc[:, BW:])).astype(BF16)
    gz_ref[0] = _dot(hx_ref[...], w_ref[:, 6 * BW:8 * BW]).astype(BF16)
    pf = _dot(hx_ref[...], w_ref[:, 8 * BW:9 * BW])
    rows = pf.shape[0] // FNET_RADIX
    for c in range(BW // LANES):
        pf_ref[c] = pf[:, c * LANES:(c + 1) * LANES]
        for n1 in range(FNET_RADIX):
            ff_ref[0, :, n1 * BW + c * LANES:n1 * BW + (c + 1) * LANES] = (
                pf_ref[c, pl.ds(n1, rows, stride=FNET_RADIX), :].astype(BF16))


def _inproj(x, gs, sh, w_small, cos_t, sin_t):
    b, l, d = x.shape
    tm = min(l, 512)
    vec = pl.BlockSpec((1, 1, d), lambda i, j: (i, 0, 0))
    tab = pl.BlockSpec((tm, BW // 2), lambda i, j: (j, 0))

    def out(c):
        return pl.BlockSpec((1, tm, c), lambda i, j: (i, j, 0))

    return pl.pallas_call(
        _inproj_kernel,
        grid=(b, l // tm),
        in_specs=[pl.BlockSpec((1, tm, d), lambda i, j: (i, j, 0)), vec, vec,
                  pl.BlockSpec((d, SMALL_COLS), lambda i, j: (0, 0)), tab, tab],
        out_specs=[out(4 * BW), out(BW), out(2 * BW),
                   pl.BlockSpec((1, tm // FNET_RADIX, FNET_RADIX * BW), lambda i, j: (i, j, 0))],
        out_shape=[jax.ShapeDtypeStruct((b, l, c), BF16) for c in (4 * BW, BW, 2 * BW)]
        + [jax.ShapeDtypeStruct((b, l // FNET_RADIX, FNET_RADIX * BW), BF16)],
        scratch_shapes=[pltpu.VMEM((tm, d), BF16), pltpu.VMEM((BW // LANES, tm, LANES), F32)],
        compiler_params=_cparams("parallel", "parallel"),
        name="inproj",
    )(x, gs, sh, w_small, cos_t, sin_t)


def _mixer_kernel(n_chunks, qkvg_ref, conf_ref, gz_ref, sf0_ref, sb0_ref,
                  mst_ref, wqf_ref, wqb_ref, wkf_ref, wkb_ref, cdf_ref, cdb_ref, bd_ref, avg_ref, gn_ref, hm_ref,
                  cw_ref, cb_ref, cg_ref, cbe_ref, mg_ref, mbe_ref, mws_ref, mbs_ref, mgm_ref,
                  out_ref, cout_ref, mout_ref, sf_ref, sb_ref, o_scr, uf_scr, ub_scr, st_scr, hp_scr):
    unroll = 4 if n_chunks % 4 == 0 else (2 if n_chunks % 2 == 0 else 1)
    l = n_chunks * CHUNK
    hp_scr[0:CONV_PAD, :] = jnp.zeros((CONV_PAD, BW), F32)
    hp_scr[CONV_PAD + l:2 * CONV_PAD + l, :] = jnp.zeros((CONV_PAD, BW), F32)

    def chunk_rows(n):
        return pl.ds(pl.multiple_of(n * CHUNK, CHUNK), CHUNK)

    def state_update(k, wk_ref, v):
        kdec = (k * wk_ref[...]).astype(BF16)
        upd = lax.dot_general(kdec, v, (((0,), (0,)), ((), ())), preferred_element_type=F32)
        return bd_ref[...] * upd

    def intra(n, carry):
        rows = chunk_rows(n)
        q = qkvg_ref[0, rows, 0:BW]
        k = qkvg_ref[0, rows, BW:2 * BW]
        v = qkvg_ref[0, rows, 2 * BW:3 * BW]
        kf = k.astype(F32)
        qst = jnp.concatenate([q * hm_ref[h] for h in range(RET_HEADS)], axis=0)
        s = lax.dot_general(qst, k, (((1,), (1,)), ((), ())), preferred_element_type=F32)
        p = (s * mst_ref[...]).astype(BF16)
        pcat = jnp.concatenate([p[h * CHUNK:(h + 1) * CHUNK] for h in range(RET_HEADS)], axis=1)
        vbd = jnp.concatenate([v * hm_ref[RET_HEADS + h] for h in range(RET_HEADS)], axis=0)
        o_scr[rows, :] = _dot(pcat, vbd)
        uf_scr[n] = state_update(kf, wkf_ref, v)
        ub_scr[n] = state_update(kf, wkb_ref, v)
        mout_ref[0, rows, :] = _gmlp_chunk(gz_ref[0, rows, :], mg_ref, mbe_ref, mws_ref, mbs_ref,
                                           mgm_ref).astype(BF16)
        hp_scr[pl.ds(pl.multiple_of(n * CHUNK + CONV_PAD, 8), CHUNK), :] = conf_ref[0, rows, :].astype(F32)
        return carry

    lax.fori_loop(0, n_chunks, intra, 0, unroll=unroll)

    sf_ref[0] = sf0_ref[0]
    sb_ref[0] = sb0_ref[0]

    def scan(i, carry):
        st_scr[i, 0:BW, :] = sf_ref[0].astype(BF16)
        sf_ref[0] = cdf_ref[...] * sf_ref[0] + uf_scr[i]
        n = n_chunks - 1 - i
        st_scr[n, BW:2 * BW, :] = sb_ref[0].astype(BF16)
        sb_ref[0] = cdb_ref[...] * sb_ref[0] + ub_scr[n]
        return carry

    lax.fori_loop(0, n_chunks, scan, 0)

    def group_mean(t):
        hi = t.astype(BF16)
        lo = (t - hi.astype(F32)).astype(BF16)
        return _dot(jnp.concatenate([hi, lo], axis=1), avg_ref[...])

    def finish(n, carry):
        rows = chunk_rows(n)
        q = qkvg_ref[0, rows, 0:BW].astype(F32)
        qdec = jnp.concatenate([(q * wqf_ref[...]).astype(BF16), (q * wqb_ref[...]).astype(BF16)], axis=1)
        o = o_scr[rows, :] + _dot(qdec, st_scr[n])
        oc = o - group_mean(o)
        y = oc * lax.rsqrt(group_mean(oc * oc) + EPS) * gn_ref[...]
        g = qkvg_ref[0, rows, 3 * BW:4 * BW].astype(F32)
        out_ref[0, rows, :] = (_silu(g) * y).astype(BF16)
        for t in range(CHUNK // CONV_TILE):
            base = pl.multiple_of(n * CHUNK + t * CONV_TILE, CONV_TILE)
            cout_ref[0, pl.ds(base, CONV_TILE), :] = _conv_tile(hp_scr, base, cw_ref, cb_ref, cg_ref,
                                                                cbe_ref).astype(BF16)
        return carry

    lax.fori_loop(0, n_chunks, finish, 0, unroll=unroll)


def _mixers(qkvg, conf_h, gz, sf0, sb0, tabs, gn, conv_consts, gmlp_consts):
    b, l, _ = qkvg.shape
    n_chunks = l // CHUNK
    st = pl.BlockSpec((1, BW, BW), lambda i: (i, 0, 0))

    def seq(c):
        return pl.BlockSpec((1, l, c), lambda i: (i, 0, 0))

    def full(a):
        return pl.BlockSpec(a.shape, lambda i: (0,) * a.ndim)

    consts = ([tabs[k] for k in ("mst", "wqf", "wqb", "wkf", "wkb", "cdf", "cdb", "bd", "avg")] + [gn, tabs["hm"]]
              + conv_consts + gmlp_consts)
    return pl.pallas_call(
        functools.partial(_mixer_kernel, n_chunks),
        grid=(b,),
        in_specs=[seq(4 * BW), seq(BW), seq(2 * BW), st, st] + [full(a) for a in consts],
        out_specs=[seq(BW), seq(BW), seq(BW), st, st],
        out_shape=[jax.ShapeDtypeStruct((b, l, BW), BF16)] * 3 + [jax.ShapeDtypeStruct((b, BW, BW), F32)] * 2,
        scratch_shapes=[pltpu.VMEM((l, BW), F32),
                        pltpu.VMEM((n_chunks, BW, BW), F32),
                        pltpu.VMEM((n_chunks, BW, BW), F32),
                        pltpu.VMEM((n_chunks, 2 * BW, BW), BF16),
                        pltpu.VMEM((l + 2 * CONV_PAD, BW), F32)],
        compiler_params=_cparams("parallel"),
        name="mixers",
    )(qkvg, conf_h, gz, sf0, sb0, *consts)


def _state_kernel(n_chunks, k_ref, v_ref, wkf_ref, wkb_ref, cdf_ref, cdb_ref, bd_ref, sf_ref, sb_ref):
    sf_ref[0] = jnp.zeros((BW, BW), F32)
    sb_ref[0] = jnp.zeros((BW, BW), F32)

    def increment(n, wk_ref):
        rows = pl.ds(pl.multiple_of(n * CHUNK, CHUNK), CHUNK)
        kdec = (k_ref[0, rows, :].astype(F32) * wk_ref[...]).astype(BF16)
        upd = lax.dot_general(kdec, v_ref[0, rows, :], (((0,), (0,)), ((), ())), preferred_element_type=F32)
        return bd_ref[...] * upd

    def scan(i, carry):
        sf_ref[0] = cdf_ref[...] * sf_ref[0] + increment(i, wkf_ref)
        sb_ref[0] = cdb_ref[...] * sb_ref[0] + increment(n_chunks - 1 - i, wkb_ref)
        return carry

    lax.fori_loop(0, n_chunks, scan, 0)


def _final_states(qkvg, tabs):
    b, l, _ = qkvg.shape
    st = pl.BlockSpec((1, BW, BW), lambda i: (i, 0, 0))

    def full(a):
        return pl.BlockSpec(a.shape, lambda i: (0,) * a.ndim)

    consts = [tabs[k] for k in ("wkf", "wkb", "cdf", "cdb", "bd")]
    return pl.pallas_call(
        functools.partial(_state_kernel, l // CHUNK),
        grid=(b,),
        in_specs=[pl.BlockSpec((1, l, BW), lambda i: (i, 0, 1)),
                  pl.BlockSpec((1, l, BW), lambda i: (i, 0, 2))] + [full(a) for a in consts],
        out_specs=[st, st],
        out_shape=[jax.ShapeDtypeStruct((b, BW, BW), F32)] * 2,
        compiler_params=_cparams("parallel"),
        name="final_states",
    )(qkvg, qkvg, *consts)


def _retention_tables(log_g):
    lf, lb = log_g[0], log_g[1]
    idx = jnp.arange(CHUNK, dtype=F32)
    diff = idx[:, None] - idx[None, :]
    m = (jnp.where(diff > 0, jnp.exp(lf[:, None, None] * jnp.maximum(diff, 0.0)), 0.0)
         + jnp.where(diff < 0, jnp.exp(lb[:, None, None] * jnp.maximum(-diff, 0.0)), 0.0)
         + jnp.where(diff == 0, 2.0, 0.0))
    lane = np.arange(BW)
    hq = (lane % (BW // 2)) // (RET_HEAD_DIM // 2)
    hv = lane // RET_HEAD_DIM
    lfq, lbq = lf[hq][None, :], lb[hq][None, :]
    i = idx[:, None]
    hm = np.concatenate([(hq[None, :] == np.arange(RET_HEADS)[:, None]),
                         (hv[None, :] == np.arange(RET_HEADS)[:, None])], axis=0).astype(np.float32)
    return dict(
        mst=m.reshape(RET_HEADS * CHUNK, CHUNK),
        wqf=jnp.exp(lfq * (i + 1.0)), wqb=jnp.exp(lbq * (CHUNK - i)),
        wkf=jnp.exp(lfq * (CHUNK - 1.0 - i)), wkb=jnp.exp(lbq * i),
        cdf=jnp.broadcast_to(jnp.exp(lf[hq] * CHUNK)[:, None], (BW, BW)),
        cdb=jnp.broadcast_to(jnp.exp(lb[hq] * CHUNK)[:, None], (BW, BW)),
        bd=jnp.asarray((hq[:, None] == hv[None, :]).astype(np.float32)),
        avg=jnp.asarray(np.tile((hv[:, None] == hv[None, :]).astype(np.float32) / RET_HEAD_DIM,
                                (2, 1))).astype(BF16),
        hm=jnp.asarray(np.broadcast_to(hm[:, None, :], (2 * RET_HEADS, CHUNK, BW))).astype(BF16),
    )


CONV_TILE = 128
CONV_PAD = 16


def _conv_tile(hp_ref, base, w_ref, b_ref, g_ref, be_ref):
    first = CONV_PAD - (CONV_KERNEL - 1) // 2
    win = hp_ref[pl.ds(base, CONV_TILE + 2 * CONV_PAD), :]
    acc = jnp.zeros((CONV_TILE, BW), F32) + b_ref[...]
    for s in range(8):
        part = None
        for m in range(4):
            k = 8 * m + s - first
            if 0 <= k < CONV_KERNEL:
                term = win[8 * m:8 * m + CONV_TILE + 8, :] * w_ref[k:k + 1, :]
                part = term if part is None else part + term
        if part is not None:
            acc = acc + part[s:s + CONV_TILE, :]
    return _silu(_layer_norm(acc, g_ref[...], be_ref[...]))


def _conv_consts(w_dw, b_dw, ln_g, ln_b):
    return [w_dw, b_dw.reshape(1, BW), ln_g.reshape(1, BW), ln_b.reshape(1, BW)]


def _gmlp_chunk(z, g_ref, be_ref, ws_ref, bs_ref, gm_ref):
    z = z.astype(F32)
    z = 0.5 * z * (1.0 + lax.erf(z * (2.0 ** -0.5)))
    u, v = z[:, :BW], z[:, BW:]
    v = _layer_norm(v, g_ref[...], be_ref[...]).astype(BF16)
    full = _dot(ws_ref[...], v)
    sv = bs_ref[...]
    for g in range(GROUPS):
        sv = sv + full[g * CHUNK:(g + 1) * CHUNK, :] * gm_ref[g:g + 1, :]
    return u * sv


def _gmlp_consts(ln_g, ln_b, ws, bs):
    gm = np.repeat(np.eye(GROUPS, dtype=np.float32), GROUP_C, axis=1)
    return [ln_g.reshape(1, BW), ln_b.reshape(1, BW), ws.reshape(GROUPS * CHUNK, CHUNK).astype(BF16),
            jnp.repeat(bs.T, GROUP_C, axis=1), jnp.asarray(gm)]


FNET_RADIX = 8


@functools.lru_cache(maxsize=None)
def _dft_tables(l):
    n2 = l // FNET_RADIX
    c = np.arange(GROUP_C, dtype=np.int64)
    angc = 2.0 * np.pi * ((c[:, None] * c[None, :]) % GROUP_C) / GROUP_C
    eye = np.eye(GROUPS)
    wcs = np.concatenate([np.kron(eye, np.cos(angc)), np.kron(eye, np.sin(angc))], axis=1)
    n = np.arange(n2, dtype=np.int64)
    ang = 2.0 * np.pi * ((n[:, None] * n[None, :]) % n2) / n2
    cc, ss = np.cos(ang), np.sin(ang)
    m1 = np.block([[cc, ss], [-ss, cc]])
    angt = 2.0 * np.pi * (n[:, None] * np.arange(FNET_RADIX)[None, :]) / l
    tc = np.repeat(np.cos(angt), BW, axis=1)
    ts = np.repeat(np.sin(angt), BW, axis=1)
    return tuple(t.astype(np.float32) for t in (wcs, m1, tc, ts))


def _fnet_kernel(n2, scale, f_ref, wcs_ref, m1_ref, tc_ref, ts_ref, out_ref, ab_ref, zr_ref, zi_ref):
    for i in range(FNET_RADIX):
        cols = slice(i * BW, (i + 1) * BW)
        cs = _dot(f_ref[0, :, cols], wcs_ref[...])
        ab_ref[0:n2, cols] = cs[:, :BW].astype(BF16)
        ab_ref[n2:2 * n2, cols] = (-cs[:, BW:]).astype(BF16)
    for i in range(FNET_RADIX):
        cols = slice(i * BW, (i + 1) * BW)
        y = _dot(m1_ref[...], ab_ref[:, cols])
        yr, yi = y[:n2], y[n2:]
        if i == 0:
            zr_ref[:, cols] = yr
            zi_ref[:, cols] = yi
        else:
            tc, ts = tc_ref[:, cols], ts_ref[:, cols]
            zr_ref[:, cols] = yr * tc + yi * ts
            zi_ref[:, cols] = yi * tc - yr * ts
    step = min(n2, 64)
    half = 0.5 ** 0.5
    for t in range(n2 // step):
        rows = slice(t * step, (t + 1) * step)
        zr = [zr_ref[rows, i * BW:(i + 1) * BW] for i in range(FNET_RADIX)]
        zi = [zi_ref[rows, i * BW:(i + 1) * BW] for i in range(FNET_RADIX)]
        a, b, c, d = zr[0] + zr[4], zr[0] - zr[4], zr[2] + zr[6], zi[2] - zi[6]
        even = (a + c, b + d, a - c, b - d)
        pr, mr = zr[1] + zr[5], zr[1] - zr[5]
        qr, nr = zr[3] + zr[7], zr[3] - zr[7]
        pi, mi = zi[1] + zi[5], zi[1] - zi[5]
        qi, ni = zi[3] + zi[7], zi[3] - zi[7]
        odd_r = (pr + qr, mr + ni, pr - qr, mr - ni)
        odd_i = (pi + qi, mi - nr, pi - qi, mi + nr)
        turned = (odd_r[0], (odd_r[1] + odd_i[1]) * half, odd_i[2], (odd_i[3] - odd_r[3]) * half)
        for k in range(FNET_RADIX // 2):
            for k1, val in ((k, even[k] + turned[k]), (k + FNET_RADIX // 2, even[k] - turned[k])):
                out_ref[0, k1 * n2 + t * step:k1 * n2 + (t + 1) * step, :] = (val * scale).astype(BF16)


def _fnet(f):
    b, n2, _ = f.shape
    l = n2 * FNET_RADIX
    wcs, m1, tc, ts = _dft_tables(l)
    scale = float(1.0 / np.sqrt(l * GROUP_C))

    def full(a):
        return pl.BlockSpec(a.shape, lambda i: (0,) * a.ndim)

    consts = [jnp.asarray(wcs).astype(BF16), jnp.asarray(m1).astype(BF16), jnp.asarray(tc), jnp.asarray(ts)]
    return pl.pallas_call(
        functools.partial(_fnet_kernel, n2, scale),
        grid=(b,),
        in_specs=[pl.BlockSpec((1, n2, FNET_RADIX * BW), lambda i: (i, 0, 0))] + [full(a) for a in consts],
        out_specs=pl.BlockSpec((1, l, BW), lambda i: (i, 0, 0)),
        out_shape=jax.ShapeDtypeStruct((b, l, BW), BF16),
        scratch_shapes=[pltpu.VMEM((2 * n2, FNET_RADIX * BW), BF16),
                        pltpu.VMEM((n2, FNET_RADIX * BW), F32),
                        pltpu.VMEM((n2, FNET_RADIX * BW), F32)],
        compiler_params=_cparams("parallel"),
        name="fnet",
    )(f, *consts)


def _merge_kernel(x_ref, gs_ref, sh_ref, ga_ref, p0_ref, p1_ref, p2_ref, p3_ref,
                  wg_ref, bg_ref, wo_ref, wout_ref, out_ref, hx_ref, m_ref):
    x = x_ref[0]
    hx_ref[...] = _modnorm(x, gs_ref[0], sh_ref[0]).astype(BF16)
    for i, p_ref in enumerate((p0_ref, p1_ref, p2_ref, p3_ref)):
        cols = slice(i * D_MODEL, (i + 1) * D_MODEL)
        gate = _sigmoid(_dot(hx_ref[...], wg_ref[:, cols]) + bg_ref[:, cols])
        term = gate * _dot(p_ref[0], wo_ref[i])
        if i == 0:
            m_ref[...] = term
        else:
            m_ref[...] += term
    out_ref[0] = x + ga_ref[0] * _dot(m_ref[...].astype(BF16), wout_ref[...])


def _merge(x, gs, sh, ga, pres, w_gate, b_gate, w_bo, w_out):
    b, l, d = x.shape
    tm = min(l, 512)
    vec = pl.BlockSpec((1, 1, d), lambda i, j: (i, 0, 0))
    pre = pl.BlockSpec((1, tm, BW), lambda i, j: (i, j, 0))
    return pl.pallas_call(
        _merge_kernel,
        grid=(b, l // tm),
        in_specs=[pl.BlockSpec((1, tm, d), lambda i, j: (i, j, 0)), vec, vec, vec, pre, pre, pre, pre,
                  pl.BlockSpec((d, 4 * d), lambda i, j: (0, 0)),
                  pl.BlockSpec((1, 4 * d), lambda i, j: (0, 0)),
                  pl.BlockSpec((4, BW, d), lambda i, j: (0, 0, 0)),
                  pl.BlockSpec((d, d), lambda i, j: (0, 0))],
        out_specs=pl.BlockSpec((1, tm, d), lambda i, j: (i, j, 0)),
        out_shape=jax.ShapeDtypeStruct((b, l, d), F32),
        scratch_shapes=[pltpu.VMEM((tm, d), BF16), pltpu.VMEM((tm, d), F32)],
        compiler_params=_cparams("parallel", "parallel"),
        name="merge",
    )(x, gs, sh, ga, *pres, w_gate, b_gate.reshape(1, 4 * d), w_bo, w_out)


FFN_TC = 256
FFN_ROWS = 256
FFN_SUB = 128


def _ffn_pad(grid_w, row_conv):
    return (grid_w if row_conv else 0) + 8


def _ffn_kernel(l, grid_w, row_conv, final_norm, x_ref, gs_ref, sh_ref, ga_ref, wa_ref, wb_ref, dw_ref,
                db_ref, wd_ref, gf_ref, out_ref, hx_ref, ac_ref, al_ref, ar_ref, b_ref, h_ref, wds_ref):
    j = pl.program_id(1)
    pad = _ffn_pad(grid_w, row_conv)
    n_tiles = l // FFN_ROWS

    @pl.when(j == 0)
    def _():
        for t in range(n_tiles):
            rows = slice(t * FFN_ROWS, (t + 1) * FFN_ROWS)
            x = x_ref[0, rows, :]
            hx_ref[rows, :] = _modnorm(x, gs_ref[0], sh_ref[0]).astype(BF16)
            out_ref[0, rows, :] = x
        for ref in (ac_ref, al_ref, ar_ref):
            ref[0:pad, :] = jnp.zeros((pad, FFN_TC), F32)
            ref[pad + l:2 * pad + l, :] = jnp.zeros((pad, FFN_TC), F32)

    wds_ref[...] = (wd_ref[...].astype(F32) * ga_ref[0]).astype(BF16)

    col = lax.broadcasted_iota(jnp.int32, (FFN_ROWS, FFN_TC), 0) % grid_w
    not_first = col != 0
    not_last = col != grid_w - 1

    step = min(l, 512)
    for t in range(l // step):
        rows = slice(t * step, (t + 1) * step)
        ac_ref[pad + t * step:pad + (t + 1) * step, :] = _dot(hx_ref[rows, :], wa_ref[...])
        b_ref[rows, :] = _dot(hx_ref[rows, :], wb_ref[...])

    for t in range(n_tiles):
        lo = pad + t * FFN_ROWS
        al_ref[lo:lo + FFN_ROWS, :] = jnp.where(not_first, ac_ref[lo - 1:lo - 1 + FFN_ROWS, :], 0.0)
        ar_ref[lo:lo + FFN_ROWS, :] = jnp.where(not_last, ac_ref[lo + 1:lo + 1 + FFN_ROWS, :], 0.0)

    for t in range(l // FFN_SUB):
        acc = jnp.zeros((FFN_SUB, FFN_TC), F32) + db_ref[...]
        for dr in ((-1, 0, 1) if row_conv else (0,)):
            off = pad + t * FFN_SUB + dr * grid_w
            tap = 3 * (dr + 1)
            acc = acc + al_ref[off:off + FFN_SUB, :] * dw_ref[tap:tap + 1, :]
            acc = acc + ac_ref[off:off + FFN_SUB, :] * dw_ref[tap + 1:tap + 2, :]
            acc = acc + ar_ref[off:off + FFN_SUB, :] * dw_ref[tap + 2:tap + 3, :]
        rows = slice(t * FFN_SUB, (t + 1) * FFN_SUB)
        h_ref[rows, :] = (_silu(acc) * b_ref[rows, :]).astype(BF16)

    for t in range(l // step):
        rows = slice(t * step, (t + 1) * step)
        out_ref[0, rows, :] += _dot(h_ref[rows, :], wds_ref[...])

    if final_norm:
        @pl.when(j == pl.num_programs(1) - 1)
        def _():
            for t in range(l // FFN_ROWS):
                rows = slice(t * FFN_ROWS, (t + 1) * FFN_ROWS)
                y = out_ref[0, rows, :]
                out_ref[0, rows, :] = y * lax.rsqrt(jnp.mean(y * y, axis=-1, keepdims=True) + EPS) * gf_ref[...]


def _conv_ffn(x, gs, sh, ga, w_up, dw, db, w_down, g_final, grid_w, row_conv, final_norm):
    b, l, d = x.shape
    nj = D_FF // FFN_TC
    pad = _ffn_pad(grid_w, row_conv)
    vec = pl.BlockSpec((1, 1, d), lambda i, j: (i, 0, 0))
    return pl.pallas_call(
        functools.partial(_ffn_kernel, l, grid_w, row_conv, final_norm),
        grid=(b, nj),
        in_specs=[pl.BlockSpec((1, l, d), lambda i, j: (i, 0, 0)), vec, vec, vec,
                  pl.BlockSpec((d, FFN_TC), lambda i, j: (0, j)),
                  pl.BlockSpec((d, FFN_TC), lambda i, j: (0, j + nj)),
                  pl.BlockSpec((9, FFN_TC), lambda i, j: (0, j)),
                  pl.BlockSpec((1, FFN_TC), lambda i, j: (0, j)),
                  pl.BlockSpec((FFN_TC, d), lambda i, j: (j, 0)),
                  pl.BlockSpec((1, d), lambda i, j: (0, 0))],
        out_specs=pl.BlockSpec((1, l, d), lambda i, j: (i, 0, 0)),
        out_shape=jax.ShapeDtypeStruct((b, l, d), F32),
        scratch_shapes=([pltpu.VMEM((l, d), BF16)] + [pltpu.VMEM((l + 2 * pad, FFN_TC), F32)] * 3
                        + [pltpu.VMEM((l, FFN_TC), F32), pltpu.VMEM((l, FFN_TC), BF16),
                           pltpu.VMEM((FFN_TC, d), BF16)]),
        compiler_params=_cparams("parallel", "arbitrary"),
        name="conv_ffn",
    )(x, gs, sh, ga, w_up, w_up, dw.reshape(9, D_FF), db.reshape(1, D_FF), w_down, g_final.reshape(1, d))


def _rope_tables(n_pos):
    half = RET_HEAD_DIM // 2
    inv_freq = ROPE_BASE ** (-jnp.arange(half, dtype=F32) / half)
    ang = jnp.arange(n_pos, dtype=jnp.int32).astype(F32)[:, None] * inv_freq[None, :]
    return jnp.tile(jnp.cos(ang), (1, RET_HEADS)), jnp.tile(jnp.sin(ang), (1, RET_HEADS))


def _split_half_perm():
    half = RET_HEAD_DIM // 2
    t, h, i = np.meshgrid(np.arange(2), np.arange(RET_HEADS), np.arange(half), indexing="ij")
    return (h * RET_HEAD_DIM + t * half + i).reshape(-1)


def kernel(x, c, ctx, c_ctx, w_ada, b_ada, g_norm1, g_norm2, w_in, b_gate, ret_decay, ret_gn, w_ret_o, conv_dw, conv_db, conv_ln_g, conv_ln_b, w_conv_o, gmlp_ln_g, gmlp_ln_b, gmlp_ws, gmlp_bs, w_gmlp_o, w_fnet_o, w_out, w_ffn_up, ffn_dw, ffn_db, w_ffn_down, g_final):
    bsz, seq, d = x.shape
    n_ctx = ctx.shape[1]
    cc = jnp.concatenate([c, c_ctx[None, :], jnp.zeros((7, d), F32)], axis=0)
    mods = _ada(cc, w_ada, b_ada)
    cos_t, sin_t = _rope_tables(n_ctx + seq)
    perm = _split_half_perm()
    zero_state = jnp.zeros((bsz, BW, BW), F32)
    xc = ctx

    def mixers(stream, l_idx, gs, sh, ga, cos_p, sin_p, sf0, sb0, tabs, want_out):
        w_small = jnp.concatenate([w_in[l_idx, :, :BW][:, perm], w_in[l_idx, :, BW:2 * BW][:, perm],
                                   w_in[l_idx, :, 2 * BW:SMALL_COLS]], axis=1).astype(BF16)
        qkvg, conf_h, gz, ff = _inproj(stream, gs, sh, w_small, cos_p, sin_p)
        if not want_out:
            return (None,) + tuple(_final_states(qkvg, tabs))
        ret_pre, conf_pre, gm_pre, sf, sb = _mixers(
            qkvg, conf_h, gz, sf0, sb0, tabs, ret_gn[l_idx].reshape(1, BW),
            _conv_consts(conv_dw[l_idx], conv_db[l_idx], conv_ln_g[l_idx], conv_ln_b[l_idx]),
            _gmlp_consts(gmlp_ln_g[l_idx], gmlp_ln_b[l_idx], gmlp_ws[l_idx], gmlp_bs[l_idx]))
        fn_pre = _fnet(ff)
        w_bo = jnp.stack([w_ret_o[l_idx], w_conv_o[l_idx], w_gmlp_o[l_idx], w_fnet_o[l_idx]]).astype(BF16)
        new = _merge(stream, gs, sh, ga, (ret_pre, conf_pre, gm_pre, fn_pre),
                     w_in[l_idx, :, GATE_OFF:].astype(BF16), b_gate[l_idx], w_bo, w_out[l_idx].astype(BF16))
        return new, sf, sb

    for l_idx in range(DEPTH):
        last = l_idx == DEPTH - 1
        lat = [m[:, None, :] for m in jnp.split(mods[l_idx, :bsz], 6, axis=-1)]
        cm = [jnp.broadcast_to(m[None, None, :], (bsz, 1, d)) for m in jnp.split(mods[l_idx, bsz], 6, axis=-1)]
        g1 = g_norm1[l_idx][None, None, :]
        g2 = g_norm2[l_idx][None, None, :]
        tabs = _retention_tables(jax.nn.log_sigmoid(ret_decay[l_idx].astype(F32)))
        w_up = w_ffn_up[l_idx].astype(BF16)
        w_down = w_ffn_down[l_idx].astype(BF16)

        xc_mix, s_f, s_b = mixers(xc, l_idx, g1 * (1 + cm[1]), cm[0], cm[2], cos_t[:n_ctx], sin_t[:n_ctx],
                                  zero_state, zero_state, tabs, not last)
        x, _, _ = mixers(x, l_idx, g1 * (1 + lat[1]), lat[0], lat[2], cos_t[n_ctx:], sin_t[n_ctx:],
                         s_f, s_b, tabs, True)
        x = _conv_ffn(x, g2 * (1 + lat[4]), lat[3], lat[5], w_up, ffn_dw[l_idx], ffn_db[l_idx], w_down,
                      g_final, GRID_W, True, last)
        if not last:
            fold = max(1, seq // n_ctx)
            while bsz % fold:
                fold //= 2
            xc = _conv_ffn(xc_mix.reshape(bsz // fold, fold * n_ctx, d), (g2 * (1 + cm[4]))[:bsz // fold],
                           cm[3][:bsz // fold], cm[5][:bsz // fold], w_up, ffn_dw[l_idx], ffn_db[l_idx],
                           w_down, g_final, n_ctx, False, False).reshape(bsz, n_ctx, d)
    return x
```

```python
import functools

import jax
import jax.numpy as jnp
import numpy as np
from jax import lax
from jax.experimental import pallas as pl
from jax.experimental.pallas import tpu as pltpu

D_MODEL = 1024
DEPTH = 2
GRID_W = 64
RET_HEADS = 4
RET_HEAD_DIM = 64
BW = 256
CHUNK = 128
LANES = 128
SUBLANES = 8
TOKEN_TILE = 512
ADA_COL_TILE = 1536
ROPE_BASE = 10000.0
CONV_KERNEL = 31
GROUPS = 4
GROUP_C = BW // GROUPS
D_FF = ((8 * D_MODEL // 3 + 127) // 128) * 128
EPS = 1e-6
SMALL_COLS = 9 * BW
GATE_OFF = SMALL_COLS

F32 = jnp.float32
BF16 = jnp.bfloat16
HIGHEST = lax.Precision.HIGHEST
VMEM_LIMIT_BYTES = 56 * 1024 * 1024


def _cparams(*sem):
    return pltpu.CompilerParams(dimension_semantics=sem, vmem_limit_bytes=VMEM_LIMIT_BYTES)


def _dot(a, b):
    return jnp.dot(a, b, preferred_element_type=F32)


def _modnorm(x, gs, sh):
    ms = jnp.mean(x * x, axis=-1, keepdims=True)
    return x * lax.rsqrt(ms + EPS) * gs + sh


def _sigmoid(x):
    return 1.0 / (1.0 + jnp.exp(-x))


def _silu(x):
    return x * _sigmoid(x)


def _layer_norm(x, g, b):
    xc = x - jnp.mean(x, axis=-1, keepdims=True)
    return xc * lax.rsqrt(jnp.mean(xc * xc, axis=-1, keepdims=True) + EPS) * g + b


def _ada_kernel(c_ref, w_ref, b_ref, o_ref):
    o_ref[0] = jnp.dot(_silu(c_ref[...]), w_ref[0], precision=HIGHEST,
                       preferred_element_type=F32) + b_ref[0]


def _ada(cc, w_ada, b_ada):
    n, d = cc.shape
    cols = w_ada.shape[-1]
    tn = ADA_COL_TILE
    return pl.pallas_call(
        _ada_kernel,
        grid=(DEPTH, cols // tn),
        in_specs=[pl.BlockSpec((n, d), lambda l, j: (0, 0)),
                  pl.BlockSpec((1, d, tn), lambda l, j: (l, 0, j)),
                  pl.BlockSpec((1, 1, tn), lambda l, j: (l, 0, j))],
        out_specs=pl.BlockSpec((1, n, tn), lambda l, j: (l, 0, j)),
        out_shape=jax.ShapeDtypeStruct((DEPTH, n, cols), F32),
        compiler_params=_cparams("parallel", "parallel"),
        name="ada",
    )(cc, w_ada, b_ada.reshape(DEPTH, 1, cols))


def _inproj_kernel(x_ref, gs_ref, sh_ref, w_ref, cos_ref, sin_ref,
                   qkvg_ref, conf_ref, gz_ref, ff_ref, hx_ref, pf_ref):
    hx_ref[...] = _modnorm(x_ref[0], gs_ref[0], sh_ref[0]).astype(BF16)
    cos = cos_ref[...]
    sin = sin_ref[...]
    half = BW // 2

    def rot(p, scale):
        p1, p2 = p[:, :half], p[:, half:]
        return jnp.concatenate([(p1 * cos - p2 * sin) * scale, (p1 * sin + p2 * cos) * scale], axis=-1)

    pq = _dot(hx_ref[...], w_ref[:, 0:BW])
    qkvg_ref[0, :, 0:BW] = rot(pq, 1.0).astype(BF16)
    pk = _dot(hx_ref[...], w_ref[:, BW:2 * BW])
    qkvg_ref[0, :, BW:2 * BW] = rot(pk, RET_HEAD_DIM ** -0.5).astype(BF16)
    pv = _dot(hx_ref[...], w_ref[:, 2 * BW:4 * BW])
    qkvg_ref[0, :, 2 * BW:4 * BW] = pv.astype(BF16)
    pc = _dot(hx_ref[...], w_ref[:, 4 * BW:6 * BW])
    conf_ref[0] = (pc[:, :BW] * _sigmoid(pc[:, BW:])).astype(BF16)
    gz_ref[0] = _dot(hx_ref[...], w_ref[:, 6 * BW:8 * BW]).astype(BF16)
    pf = _dot(hx_ref[...], w_ref[:, 8 * BW:9 * BW])
    rows = pf.shape[0] // FNET_RADIX
    for c in range(BW // LANES):
        pf_ref[c] = pf[:, c * LANES:(c + 1) * LANES]
        for n1 in range(FNET_RADIX):
            ff_ref[0, :, n1 * BW + c * LANES:n1 * BW + (c + 1) * LANES] = (
                pf_ref[c, pl.ds(n1, rows, stride=FNET_RADIX), :].astype(BF16))


def _inproj(x, gs, sh, w_small, cos_t, sin_t):
    b, l, d = x.shape
    tm = min(l, TOKEN_TILE)
    vec = pl.BlockSpec((1, 1, d), lambda i, j: (i, 0, 0))
    tab = pl.BlockSpec((tm, BW // 2), lambda i, j: (j, 0))

    def out(c):
        return pl.BlockSpec((1, tm, c), lambda i, j: (i, j, 0))

    return pl.pallas_call(
        _inproj_kernel,
        grid=(b, l // tm),
        in_specs=[pl.BlockSpec((1, tm, d), lambda i, j: (i, j, 0)), vec, vec,
                  pl.BlockSpec((d, SMALL_COLS), lambda i, j: (0, 0)), tab, tab],
        out_specs=[out(4 * BW), out(BW), out(2 * BW),
                   pl.BlockSpec((1, tm // FNET_RADIX, FNET_RADIX * BW), lambda i, j: (i, j, 0))],
        out_shape=[jax.ShapeDtypeStruct((b, l, c), BF16) for c in (4 * BW, BW, 2 * BW)]
        + [jax.ShapeDtypeStruct((b, l // FNET_RADIX, FNET_RADIX * BW), BF16)],
        scratch_shapes=[pltpu.VMEM((tm, d), BF16), pltpu.VMEM((BW // LANES, tm, LANES), F32)],
        compiler_params=_cparams("parallel", "parallel"),
        name="inproj",
    )(x, gs, sh, w_small, cos_t, sin_t)


def _mixer_kernel(n_chunks, qkvg_ref, conf_ref, gz_ref, sf0_ref, sb0_ref,
                  mst_ref, wqf_ref, wqb_ref, wkf_ref, wkb_ref, cdf_ref, cdb_ref, bd_ref, avg_ref, gn_ref, hm_ref,
                  cw_ref, cb_ref, cg_ref, cbe_ref, mg_ref, mbe_ref, mws_ref, mbs_ref, mgm_ref,
                  out_ref, cout_ref, mout_ref, sf_ref, sb_ref, o_scr, uf_scr, ub_scr, st_scr, hp_scr):
    unroll = 4 if n_chunks % 4 == 0 else (2 if n_chunks % 2 == 0 else 1)
    l = n_chunks * CHUNK
    hp_scr[0:CONV_PAD, :] = jnp.zeros((CONV_PAD, BW), F32)
    hp_scr[CONV_PAD + l:2 * CONV_PAD + l, :] = jnp.zeros((CONV_PAD, BW), F32)

    def chunk_rows(n):
        return pl.ds(pl.multiple_of(n * CHUNK, CHUNK), CHUNK)

    def state_update(k, wk_ref, v):
        kdec = (k * wk_ref[...]).astype(BF16)
        upd = lax.dot_general(kdec, v, (((0,), (0,)), ((), ())), preferred_element_type=F32)
        return bd_ref[...] * upd

    def intra(n, carry):
        rows = chunk_rows(n)
        q = qkvg_ref[0, rows, 0:BW]
        k = qkvg_ref[0, rows, BW:2 * BW]
        v = qkvg_ref[0, rows, 2 * BW:3 * BW]
        kf = k.astype(F32)
        qst = jnp.concatenate([q * hm_ref[h] for h in range(RET_HEADS)], axis=0)
        s = lax.dot_general(qst, k, (((1,), (1,)), ((), ())), preferred_element_type=F32)
        p = (s * mst_ref[...]).astype(BF16)
        pcat = jnp.concatenate([p[h * CHUNK:(h + 1) * CHUNK] for h in range(RET_HEADS)], axis=1)
        vbd = jnp.concatenate([v * hm_ref[RET_HEADS + h] for h in range(RET_HEADS)], axis=0)
        o_scr[rows, :] = _dot(pcat, vbd)
        uf_scr[n] = state_update(kf, wkf_ref, v)
        ub_scr[n] = state_update(kf, wkb_ref, v)
        mout_ref[0, rows, :] = _gmlp_chunk(gz_ref[0, rows, :], mg_ref, mbe_ref, mws_ref, mbs_ref,
                                           mgm_ref).astype(BF16)
        hp_scr[pl.ds(pl.multiple_of(n * CHUNK + CONV_PAD, SUBLANES), CHUNK), :] = conf_ref[0, rows, :].astype(F32)
        return carry

    lax.fori_loop(0, n_chunks, intra, 0, unroll=unroll)

    sf_ref[0] = sf0_ref[0]
    sb_ref[0] = sb0_ref[0]

    def scan(i, carry):
        st_scr[i, 0:BW, :] = sf_ref[0].astype(BF16)
        sf_ref[0] = cdf_ref[...] * sf_ref[0] + uf_scr[i]
        n = n_chunks - 1 - i
        st_scr[n, BW:2 * BW, :] = sb_ref[0].astype(BF16)
        sb_ref[0] = cdb_ref[...] * sb_ref[0] + ub_scr[n]
        return carry

    lax.fori_loop(0, n_chunks, scan, 0)

    def group_mean(t):
        hi = t.astype(BF16)
        lo = (t - hi.astype(F32)).astype(BF16)
        return _dot(jnp.concatenate([hi, lo], axis=1), avg_ref[...])

    def finish(n, carry):
        rows = chunk_rows(n)
        q = qkvg_ref[0, rows, 0:BW].astype(F32)
        qdec = jnp.concatenate([(q * wqf_ref[...]).astype(BF16), (q * wqb_ref[...]).astype(BF16)], axis=1)
        o = o_scr[rows, :] + _dot(qdec, st_scr[n])
        oc = o - group_mean(o)
        y = oc * lax.rsqrt(group_mean(oc * oc) + EPS) * gn_ref[...]
        g = qkvg_ref[0, rows, 3 * BW:4 * BW].astype(F32)
        out_ref[0, rows, :] = (_silu(g) * y).astype(BF16)
        for t in range(CHUNK // CONV_TILE):
            base = pl.multiple_of(n * CHUNK + t * CONV_TILE, CONV_TILE)
            cout_ref[0, pl.ds(base, CONV_TILE), :] = _conv_tile(hp_scr, base, cw_ref, cb_ref, cg_ref,
                                                                cbe_ref).astype(BF16)
        return carry

    lax.fori_loop(0, n_chunks, finish, 0, unroll=unroll)


def _mixers(qkvg, conf_h, gz, sf0, sb0, tabs, gn, conv_consts, gmlp_consts):
    b, l, _ = qkvg.shape
    n_chunks = l // CHUNK
    st = pl.BlockSpec((1, BW, BW), lambda i: (i, 0, 0))

    def seq(c):
        return pl.BlockSpec((1, l, c), lambda i: (i, 0, 0))

    def full(a):
        return pl.BlockSpec(a.shape, lambda i: (0,) * a.ndim)

    consts = ([tabs[k] for k in ("mst", "wqf", "wqb", "wkf", "wkb", "cdf", "cdb", "bd", "avg")] + [gn, tabs["hm"]]
              + conv_consts + gmlp_consts)
    return pl.pallas_call(
        functools.partial(_mixer_kernel, n_chunks),
        grid=(b,),
        in_specs=[seq(4 * BW), seq(BW), seq(2 * BW), st, st] + [full(a) for a in consts],
        out_specs=[seq(BW), seq(BW), seq(BW), st, st],
        out_shape=[jax.ShapeDtypeStruct((b, l, BW), BF16)] * 3 + [jax.ShapeDtypeStruct((b, BW, BW), F32)] * 2,
        scratch_shapes=[pltpu.VMEM((l, BW), F32),
                        pltpu.VMEM((n_chunks, BW, BW), F32),
                        pltpu.VMEM((n_chunks, BW, BW), F32),
                        pltpu.VMEM((n_chunks, 2 * BW, BW), BF16),
                        pltpu.VMEM((l + 2 * CONV_PAD, BW), F32)],
        compiler_params=_cparams("parallel"),
        name="mixers",
    )(qkvg, conf_h, gz, sf0, sb0, *consts)


def _state_kernel(n_chunks, k_ref, v_ref, wkf_ref, wkb_ref, cdf_ref, cdb_ref, bd_ref, sf_ref, sb_ref):
    sf_ref[0] = jnp.zeros((BW, BW), F32)
    sb_ref[0] = jnp.zeros((BW, BW), F32)

    def increment(n, wk_ref):
        rows = pl.ds(pl.multiple_of(n * CHUNK, CHUNK), CHUNK)
        kdec = (k_ref[0, rows, :].astype(F32) * wk_ref[...]).astype(BF16)
        upd = lax.dot_general(kdec, v_ref[0, rows, :], (((0,), (0,)), ((), ())), preferred_element_type=F32)
        return bd_ref[...] * upd

    def scan(i, carry):
        sf_ref[0] = cdf_ref[...] * sf_ref[0] + increment(i, wkf_ref)
        sb_ref[0] = cdb_ref[...] * sb_ref[0] + increment(n_chunks - 1 - i, wkb_ref)
        return carry

    lax.fori_loop(0, n_chunks, scan, 0)


def _final_states(qkvg, tabs):
    b, l, _ = qkvg.shape
    st = pl.BlockSpec((1, BW, BW), lambda i: (i, 0, 0))

    def full(a):
        return pl.BlockSpec(a.shape, lambda i: (0,) * a.ndim)

    consts = [tabs[k] for k in ("wkf", "wkb", "cdf", "cdb", "bd")]
    return pl.pallas_call(
        functools.partial(_state_kernel, l // CHUNK),
        grid=(b,),
        in_specs=[pl.BlockSpec((1, l, BW), lambda i: (i, 0, 1)),
                  pl.BlockSpec((1, l, BW), lambda i: (i, 0, 2))] + [full(a) for a in consts],
        out_specs=[st, st],
        out_shape=[jax.ShapeDtypeStruct((b, BW, BW), F32)] * 2,
        compiler_params=_cparams("parallel"),
        name="final_states",
    )(qkvg, qkvg, *consts)


def _retention_tables(log_g):
    lf, lb = log_g[0], log_g[1]
    idx = jnp.arange(CHUNK, dtype=F32)
    diff = idx[:, None] - idx[None, :]
    m = (jnp.where(diff > 0, jnp.exp(lf[:, None, None] * jnp.maximum(diff, 0.0)), 0.0)
         + jnp.where(diff < 0, jnp.exp(lb[:, None, None] * jnp.maximum(-diff, 0.0)), 0.0)
         + jnp.where(diff == 0, 2.0, 0.0))
    lane = np.arange(BW)
    hq = (lane % (BW // 2)) // (RET_HEAD_DIM // 2)
    hv = lane // RET_HEAD_DIM
    lfq, lbq = lf[hq][None, :], lb[hq][None, :]
    i = idx[:, None]
    hm = np.concatenate([(hq[None, :] == np.arange(RET_HEADS)[:, None]),
                         (hv[None, :] == np.arange(RET_HEADS)[:, None])], axis=0).astype(np.float32)
    return dict(
        mst=m.reshape(RET_HEADS * CHUNK, CHUNK),
        wqf=jnp.exp(lfq * (i + 1.0)), wqb=jnp.exp(lbq * (CHUNK - i)),
        wkf=jnp.exp(lfq * (CHUNK - 1.0 - i)), wkb=jnp.exp(lbq * i),
        cdf=jnp.broadcast_to(jnp.exp(lf[hq] * CHUNK)[:, None], (BW, BW)),
        cdb=jnp.broadcast_to(jnp.exp(lb[hq] * CHUNK)[:, None], (BW, BW)),
        bd=jnp.asarray((hq[:, None] == hv[None, :]).astype(np.float32)),
        avg=jnp.asarray(np.tile((hv[:, None] == hv[None, :]).astype(np.float32) / RET_HEAD_DIM,
                                (2, 1))).astype(BF16),
        hm=jnp.asarray(np.broadcast_to(hm[:, None, :], (2 * RET_HEADS, CHUNK, BW))).astype(BF16),
    )


CONV_TILE = 128
CONV_PAD = 16


def _conv_tile(hp_ref, base, w_ref, b_ref, g_ref, be_ref):
    first = CONV_PAD - (CONV_KERNEL - 1) // 2
    win = hp_ref[pl.ds(base, CONV_TILE + 2 * CONV_PAD), :]
    acc = jnp.zeros((CONV_TILE, BW), F32) + b_ref[...]
    for s in range(SUBLANES):
        part = None
        for m in range(2 * CONV_PAD // SUBLANES):
            k = SUBLANES * m + s - first
            if 0 <= k < CONV_KERNEL:
                term = win[SUBLANES * m:SUBLANES * m + CONV_TILE + SUBLANES, :] * w_ref[k:k + 1, :]
                part = term if part is None else part + term
        if part is not None:
            acc = acc + part[s:s + CONV_TILE, :]
    return _silu(_layer_norm(acc, g_ref[...], be_ref[...]))


def _conv_consts(w_dw, b_dw, ln_g, ln_b):
    return [w_dw, b_dw.reshape(1, BW), ln_g.reshape(1, BW), ln_b.reshape(1, BW)]


def _gmlp_chunk(z, g_ref, be_ref, ws_ref, bs_ref, gm_ref):
    z = z.astype(F32)
    z = 0.5 * z * (1.0 + lax.erf(z * (2.0 ** -0.5)))
    u, v = z[:, :BW], z[:, BW:]
    v = _layer_norm(v, g_ref[...], be_ref[...]).astype(BF16)
    full = _dot(ws_ref[...], v)
    sv = bs_ref[...]
    for g in range(GROUPS):
        sv = sv + full[g * CHUNK:(g + 1) * CHUNK, :] * gm_ref[g:g + 1, :]
    return u * sv


def _gmlp_consts(ln_g, ln_b, ws, bs):
    gm = np.repeat(np.eye(GROUPS, dtype=np.float32), GROUP_C, axis=1)
    return [ln_g.reshape(1, BW), ln_b.reshape(1, BW), ws.reshape(GROUPS * CHUNK, CHUNK).astype(BF16),
            jnp.repeat(bs.T, GROUP_C, axis=1), jnp.asarray(gm)]


FNET_RADIX = 8


@functools.lru_cache(maxsize=None)
def _dft_tables(l):
    n2 = l // FNET_RADIX
    c = np.arange(GROUP_C, dtype=np.int64)
    angc = 2.0 * np.pi * ((c[:, None] * c[None, :]) % GROUP_C) / GROUP_C
    eye = np.eye(GROUPS)
    wcs = np.concatenate([np.kron(eye, np.cos(angc)), np.kron(eye, np.sin(angc))], axis=1)
    n = np.arange(n2, dtype=np.int64)
    ang = 2.0 * np.pi * ((n[:, None] * n[None, :]) % n2) / n2
    cc, ss = np.cos(ang), np.sin(ang)
    m1 = np.block([[cc, ss], [-ss, cc]])
    angt = 2.0 * np.pi * (n[:, None] * np.arange(FNET_RADIX)[None, :]) / l
    tc = np.repeat(np.cos(angt), BW, axis=1)
    ts = np.repeat(np.sin(angt), BW, axis=1)
    return tuple(t.astype(np.float32) for t in (wcs, m1, tc, ts))


def _fnet_kernel(n2, scale, f_ref, wcs_ref, m1_ref, tc_ref, ts_ref, out_ref, ab_ref, zr_ref, zi_ref):
    for i in range(FNET_RADIX):
        cols = slice(i * BW, (i + 1) * BW)
        cs = _dot(f_ref[0, :, cols], wcs_ref[...])
        ab_ref[0:n2, cols] = cs[:, :BW].astype(BF16)
        ab_ref[n2:2 * n2, cols] = (-cs[:, BW:]).astype(BF16)
    for i in range(FNET_RADIX):
        cols = slice(i * BW, (i + 1) * BW)
        y = _dot(m1_ref[...], ab_ref[:, cols])
        yr, yi = y[:n2], y[n2:]
        if i == 0:
            zr_ref[:, cols] = yr
            zi_ref[:, cols] = yi
        else:
            tc, ts = tc_ref[:, cols], ts_ref[:, cols]
            zr_ref[:, cols] = yr * tc + yi * ts
            zi_ref[:, cols] = yi * tc - yr * ts
    step = min(n2, 64)
    half = 0.5 ** 0.5
    for t in range(n2 // step):
        rows = slice(t * step, (t + 1) * step)
        zr = [zr_ref[rows, i * BW:(i + 1) * BW] for i in range(FNET_RADIX)]
        zi = [zi_ref[rows, i * BW:(i + 1) * BW] for i in range(FNET_RADIX)]
        a, b, c, d = zr[0] + zr[4], zr[0] - zr[4], zr[2] + zr[6], zi[2] - zi[6]
        even = (a + c, b + d, a - c, b - d)
        pr, mr = zr[1] + zr[5], zr[1] - zr[5]
        qr, nr = zr[3] + zr[7], zr[3] - zr[7]
        pim, mi = zi[1] + zi[5], zi[1] - zi[5]
        qi, ni = zi[3] + zi[7], zi[3] - zi[7]
        odd_r = (pr + qr, mr + ni, pr - qr, mr - ni)
        odd_i = (pim + qi, mi - nr, pim - qi, mi + nr)
        turned = (odd_r[0], (odd_r[1] + odd_i[1]) * half, odd_i[2], (odd_i[3] - odd_r[3]) * half)
        for k in range(FNET_RADIX // 2):
            for k1, val in ((k, even[k] + turned[k]), (k + FNET_RADIX // 2, even[k] - turned[k])):
                out_ref[0, k1 * n2 + t * step:k1 * n2 + (t + 1) * step, :] = (val * scale).astype(BF16)


def _fnet(f):
    b, n2, _ = f.shape
    l = n2 * FNET_RADIX
    wcs, m1, tc, ts = _dft_tables(l)
    scale = float(1.0 / np.sqrt(l * GROUP_C))

    def full(a):
        return pl.BlockSpec(a.shape, lambda i: (0,) * a.ndim)

    consts = [jnp.asarray(wcs).astype(BF16), jnp.asarray(m1).astype(BF16), jnp.asarray(tc), jnp.asarray(ts)]
    return pl.pallas_call(
        functools.partial(_fnet_kernel, n2, scale),
        grid=(b,),
        in_specs=[pl.BlockSpec((1, n2, FNET_RADIX * BW), lambda i: (i, 0, 0))] + [full(a) for a in consts],
        out_specs=pl.BlockSpec((1, l, BW), lambda i: (i, 0, 0)),
        out_shape=jax.ShapeDtypeStruct((b, l, BW), BF16),
        scratch_shapes=[pltpu.VMEM((2 * n2, FNET_RADIX * BW), BF16),
                        pltpu.VMEM((n2, FNET_RADIX * BW), F32),
                        pltpu.VMEM((n2, FNET_RADIX * BW), F32)],
        compiler_params=_cparams("parallel"),
        name="fnet",
    )(f, *consts)


def _merge_kernel(x_ref, gs_ref, sh_ref, ga_ref, p0_ref, p1_ref, p2_ref, p3_ref,
                  wg_ref, bg_ref, wo_ref, wout_ref, out_ref, hx_ref, m_ref):
    x = x_ref[0]
    hx_ref[...] = _modnorm(x, gs_ref[0], sh_ref[0]).astype(BF16)
    for i, p_ref in enumerate((p0_ref, p1_ref, p2_ref, p3_ref)):
        cols = slice(i * D_MODEL, (i + 1) * D_MODEL)
        gate = _sigmoid(_dot(hx_ref[...], wg_ref[:, cols]) + bg_ref[:, cols])
        term = gate * _dot(p_ref[0], wo_ref[i])
        if i == 0:
            m_ref[...] = term
        else:
            m_ref[...] += term
    out_ref[0] = x + ga_ref[0] * _dot(m_ref[...].astype(BF16), wout_ref[...])


def _merge(x, gs, sh, ga, pres, w_gate, b_gate, w_bo, w_out):
    b, l, d = x.shape
    tm = min(l, TOKEN_TILE)
    vec = pl.BlockSpec((1, 1, d), lambda i, j: (i, 0, 0))
    pre = pl.BlockSpec((1, tm, BW), lambda i, j: (i, j, 0))
    return pl.pallas_call(
        _merge_kernel,
        grid=(b, l // tm),
        in_specs=[pl.BlockSpec((1, tm, d), lambda i, j: (i, j, 0)), vec, vec, vec, pre, pre, pre, pre,
                  pl.BlockSpec((d, 4 * d), lambda i, j: (0, 0)),
                  pl.BlockSpec((1, 4 * d), lambda i, j: (0, 0)),
                  pl.BlockSpec((4, BW, d), lambda i, j: (0, 0, 0)),
                  pl.BlockSpec((d, d), lambda i, j: (0, 0))],
        out_specs=pl.BlockSpec((1, tm, d), lambda i, j: (i, j, 0)),
        out_shape=jax.ShapeDtypeStruct((b, l, d), F32),
        scratch_shapes=[pltpu.VMEM((tm, d), BF16), pltpu.VMEM((tm, d), F32)],
        compiler_params=_cparams("parallel", "parallel"),
        name="merge",
    )(x, gs, sh, ga, *pres, w_gate, b_gate.reshape(1, 4 * d), w_bo, w_out)


FFN_TC = 256
FFN_ROWS = 256
FFN_SUB = 128


def _ffn_pad(grid_w, row_conv):
    return (grid_w if row_conv else 0) + SUBLANES


def _ffn_kernel(l, grid_w, row_conv, final_norm, x_ref, gs_ref, sh_ref, ga_ref, wa_ref, wb_ref, dw_ref,
                db_ref, wd_ref, gf_ref, out_ref, hx_ref, ac_ref, al_ref, ar_ref, b_ref, h_ref, wds_ref):
    j = pl.program_id(1)
    pad = _ffn_pad(grid_w, row_conv)
    n_tiles = l // FFN_ROWS

    @pl.when(j == 0)
    def _():
        for t in range(n_tiles):
            rows = slice(t * FFN_ROWS, (t + 1) * FFN_ROWS)
            x = x_ref[0, rows, :]
            hx_ref[rows, :] = _modnorm(x, gs_ref[0], sh_ref[0]).astype(BF16)
            out_ref[0, rows, :] = x
        for ref in (ac_ref, al_ref, ar_ref):
            ref[0:pad, :] = jnp.zeros((pad, FFN_TC), F32)
            ref[pad + l:2 * pad + l, :] = jnp.zeros((pad, FFN_TC), F32)

    wds_ref[...] = (wd_ref[...].astype(F32) * ga_ref[0]).astype(BF16)

    col = lax.broadcasted_iota(jnp.int32, (FFN_ROWS, FFN_TC), 0) % grid_w
    not_first = col != 0
    not_last = col != grid_w - 1

    step = min(l, TOKEN_TILE)
    for t in range(l // step):
        rows = slice(t * step, (t + 1) * step)
        ac_ref[pad + t * step:pad + (t + 1) * step, :] = _dot(hx_ref[rows, :], wa_ref[...])
        b_ref[rows, :] = _dot(hx_ref[rows, :], wb_ref[...])

    for t in range(n_tiles):
        lo = pad + t * FFN_ROWS
        al_ref[lo:lo + FFN_ROWS, :] = jnp.where(not_first, ac_ref[lo - 1:lo - 1 + FFN_ROWS, :], 0.0)
        ar_ref[lo:lo + FFN_ROWS, :] = jnp.where(not_last, ac_ref[lo + 1:lo + 1 + FFN_ROWS, :], 0.0)

    for t in range(l // FFN_SUB):
        acc = jnp.zeros((FFN_SUB, FFN_TC), F32) + db_ref[...]
        for dr in ((-1, 0, 1) if row_conv else (0,)):
            off = pad + t * FFN_SUB + dr * grid_w
            tap = 3 * (dr + 1)
            acc = acc + al_ref[off:off + FFN_SUB, :] * dw_ref[tap:tap + 1, :]
            acc = acc + ac_ref[off:off + FFN_SUB, :] * dw_ref[tap + 1:tap + 2, :]
            acc = acc + ar_ref[off:off + FFN_SUB, :] * dw_ref[tap + 2:tap + 3, :]
        rows = slice(t * FFN_SUB, (t + 1) * FFN_SUB)
        h_ref[rows, :] = (_silu(acc) * b_ref[rows, :]).astype(BF16)

    for t in range(l // step):
        rows = slice(t * step, (t + 1) * step)
        out_ref[0, rows, :] += _dot(h_ref[rows, :], wds_ref[...])

    if final_norm:
        @pl.when(j == pl.num_programs(1) - 1)
        def _():
            for t in range(l // FFN_ROWS):
                rows = slice(t * FFN_ROWS, (t + 1) * FFN_ROWS)
                y = out_ref[0, rows, :]
                out_ref[0, rows, :] = y * lax.rsqrt(jnp.mean(y * y, axis=-1, keepdims=True) + EPS) * gf_ref[...]


def _conv_ffn(x, gs, sh, ga, w_up, dw, db, w_down, g_final, grid_w, row_conv, final_norm):
    b, l, d = x.shape
    nj = D_FF // FFN_TC
    pad = _ffn_pad(grid_w, row_conv)
    vec = pl.BlockSpec((1, 1, d), lambda i, j: (i, 0, 0))
    return pl.pallas_call(
        functools.partial(_ffn_kernel, l, grid_w, row_conv, final_norm),
        grid=(b, nj),
        in_specs=[pl.BlockSpec((1, l, d), lambda i, j: (i, 0, 0)), vec, vec, vec,
                  pl.BlockSpec((d, FFN_TC), lambda i, j: (0, j)),
                  pl.BlockSpec((d, FFN_TC), lambda i, j: (0, j + nj)),
                  pl.BlockSpec((9, FFN_TC), lambda i, j: (0, j)),
                  pl.BlockSpec((1, FFN_TC), lambda i, j: (0, j)),
                  pl.BlockSpec((FFN_TC, d), lambda i, j: (j, 0)),
                  pl.BlockSpec((1, d), lambda i, j: (0, 0))],
        out_specs=pl.BlockSpec((1, l, d), lambda i, j: (i, 0, 0)),
        out_shape=jax.ShapeDtypeStruct((b, l, d), F32),
        scratch_shapes=([pltpu.VMEM((l, d), BF16)] + [pltpu.VMEM((l + 2 * pad, FFN_TC), F32)] * 3
                        + [pltpu.VMEM((l, FFN_TC), F32), pltpu.VMEM((l, FFN_TC), BF16),
                           pltpu.VMEM((FFN_TC, d), BF16)]),
        compiler_params=_cparams("parallel", "arbitrary"),
        name="conv_ffn",
    )(x, gs, sh, ga, w_up, w_up, dw.reshape(9, D_FF), db.reshape(1, D_FF), w_down, g_final.reshape(1, d))


def _rope_tables(n_pos):
    half = RET_HEAD_DIM // 2
    inv_freq = ROPE_BASE ** (-jnp.arange(half, dtype=F32) / half)
    ang = jnp.arange(n_pos, dtype=jnp.int32).astype(F32)[:, None] * inv_freq[None, :]
    return jnp.tile(jnp.cos(ang), (1, RET_HEADS)), jnp.tile(jnp.sin(ang), (1, RET_HEADS))


def _split_half_perm():
    half = RET_HEAD_DIM // 2
    t, h, i = np.meshgrid(np.arange(2), np.arange(RET_HEADS), np.arange(half), indexing="ij")
    return (h * RET_HEAD_DIM + t * half + i).reshape(-1)


def kernel(x, c, ctx, c_ctx, w_ada, b_ada, g_norm1, g_norm2, w_in, b_gate, ret_decay, ret_gn, w_ret_o, conv_dw, conv_db, conv_ln_g, conv_ln_b, w_conv_o, gmlp_ln_g, gmlp_ln_b, gmlp_ws, gmlp_bs, w_gmlp_o, w_fnet_o, w_out, w_ffn_up, ffn_dw, ffn_db, w_ffn_down, g_final):
    bsz, seq, d = x.shape
    n_ctx = ctx.shape[1]
    pad_rows = -(bsz + 1) % SUBLANES
    cc = jnp.concatenate([c, c_ctx[None, :], jnp.zeros((pad_rows, d), F32)], axis=0)
    mods = _ada(cc, w_ada, b_ada)
    cos_t, sin_t = _rope_tables(n_ctx + seq)
    perm = _split_half_perm()
    zero_state = jnp.zeros((bsz, BW, BW), F32)
    xc = ctx

    def mixers(stream, l_idx, gs, sh, ga, cos_p, sin_p, sf0, sb0, tabs, want_out):
        w_small = jnp.concatenate([w_in[l_idx, :, :BW][:, perm], w_in[l_idx, :, BW:2 * BW][:, perm],
                                   w_in[l_idx, :, 2 * BW:SMALL_COLS]], axis=1).astype(BF16)
        qkvg, conf_h, gz, ff = _inproj(stream, gs, sh, w_small, cos_p, sin_p)
        if not want_out:
            return (None,) + tuple(_final_states(qkvg, tabs))
        ret_pre, conf_pre, gm_pre, sf, sb = _mixers(
            qkvg, conf_h, gz, sf0, sb0, tabs, ret_gn[l_idx].reshape(1, BW),
            _conv_consts(conv_dw[l_idx], conv_db[l_idx], conv_ln_g[l_idx], conv_ln_b[l_idx]),
            _gmlp_consts(gmlp_ln_g[l_idx], gmlp_ln_b[l_idx], gmlp_ws[l_idx], gmlp_bs[l_idx]))
        fn_pre = _fnet(ff)
        w_bo = jnp.stack([w_ret_o[l_idx], w_conv_o[l_idx], w_gmlp_o[l_idx], w_fnet_o[l_idx]]).astype(BF16)
        new = _merge(stream, gs, sh, ga, (ret_pre, conf_pre, gm_pre, fn_pre),
                     w_in[l_idx, :, GATE_OFF:].astype(BF16), b_gate[l_idx], w_bo, w_out[l_idx].astype(BF16))
        return new, sf, sb

    for l_idx in range(DEPTH):
        last = l_idx == DEPTH - 1
        lat = [m[:, None, :] for m in jnp.split(mods[l_idx, :bsz], 6, axis=-1)]
        cm = [jnp.broadcast_to(m[None, None, :], (bsz, 1, d)) for m in jnp.split(mods[l_idx, bsz], 6, axis=-1)]
        g1 = g_norm1[l_idx][None, None, :]
        g2 = g_norm2[l_idx][None, None, :]
        tabs = _retention_tables(jax.nn.log_sigmoid(ret_decay[l_idx].astype(F32)))
        w_up = w_ffn_up[l_idx].astype(BF16)
        w_down = w_ffn_down[l_idx].astype(BF16)

        xc_mix, s_f, s_b = mixers(xc, l_idx, g1 * (1 + cm[1]), cm[0], cm[2], cos_t[:n_ctx], sin_t[:n_ctx],
                                  zero_state, zero_state, tabs, not last)
        x, _, _ = mixers(x, l_idx, g1 * (1 + lat[1]), lat[0], lat[2], cos_t[n_ctx:], sin_t[n_ctx:],
                         s_f, s_b, tabs, True)
        x = _conv_ffn(x, g2 * (1 + lat[4]), lat[3], lat[5], w_up, ffn_dw[l_idx], ffn_db[l_idx], w_down,
                      g_final, GRID_W, True, last)
        if not last:
            fold = max(1, seq // n_ctx)
            while bsz % fold:
                fold //= 2
            xc = _conv_ffn(xc_mix.reshape(bsz // fold, fold * n_ctx, d), (g2 * (1 + cm[4]))[:bsz // fold],
                           cm[3][:bsz // fold], cm[5][:bsz // fold], w_up, ffn_dw[l_idx], ffn_db[l_idx],
                           w_down, g_final, n_ctx, False, False).reshape(bsz, n_ctx, d)
    return x
```

```python
import functools

import jax
import jax.numpy as jnp
import numpy as np
from jax import lax
from jax.experimental import pallas as pl
from jax.experimental.pallas import tpu as pltpu

D_MODEL = 1024
DEPTH = 2
GRID_W = 64
RET_HEADS = 4
RET_HEAD_DIM = 64
BW = 256
CHUNK = 128
LANES = 128
SUBLANES = 8
TOKEN_TILE = 512
ADA_COL_TILE = 1536
ROPE_BASE = 10000.0
CONV_KERNEL = 31
GROUPS = 4
GROUP_C = BW // GROUPS
D_FF = ((8 * D_MODEL // 3 + 127) // 128) * 128
EPS = 1e-6
SMALL_COLS = 9 * BW
GATE_OFF = SMALL_COLS

F32 = jnp.float32
BF16 = jnp.bfloat16
HIGHEST = lax.Precision.HIGHEST
VMEM_LIMIT_BYTES = 56 * 1024 * 1024


def _cparams(*sem):
    return pltpu.CompilerParams(dimension_semantics=sem, vmem_limit_bytes=VMEM_LIMIT_BYTES)


def _dot(a, b):
    return jnp.dot(a, b, preferred_element_type=F32)


def _modnorm(x, gs, sh):
    ms = jnp.mean(x * x, axis=-1, keepdims=True)
    return x * lax.rsqrt(ms + EPS) * gs + sh


def _sigmoid(x):
    return 1.0 / (1.0 + jnp.exp(-x))


def _silu(x):
    return x * _sigmoid(x)


def _layer_norm(x, g, b):
    xc = x - jnp.mean(x, axis=-1, keepdims=True)
    return xc * lax.rsqrt(jnp.mean(xc * xc, axis=-1, keepdims=True) + EPS) * g + b


def _ada_kernel(c_ref, w_ref, b_ref, o_ref):
    o_ref[0] = jnp.dot(_silu(c_ref[...]), w_ref[0], precision=HIGHEST,
                       preferred_element_type=F32) + b_ref[0]


def _ada(cc, w_ada, b_ada):
    n, d = cc.shape
    cols = w_ada.shape[-1]
    tn = ADA_COL_TILE
    return pl.pallas_call(
        _ada_kernel,
        grid=(DEPTH, cols // tn),
        in_specs=[pl.BlockSpec((n, d), lambda l, j: (0, 0)),
                  pl.BlockSpec((1, d, tn), lambda l, j: (l, 0, j)),
                  pl.BlockSpec((1, 1, tn), lambda l, j: (l, 0, j))],
        out_specs=pl.BlockSpec((1, n, tn), lambda l, j: (l, 0, j)),
        out_shape=jax.ShapeDtypeStruct((DEPTH, n, cols), F32),
        compiler_params=_cparams("parallel", "parallel"),
        name="ada",
    )(cc, w_ada, b_ada.reshape(DEPTH, 1, cols))


def _inproj_kernel(x_ref, gs_ref, sh_ref, w_ref, cos_ref, sin_ref,
                   qkvg_ref, conf_ref, gz_ref, ff_ref, hx_ref, pf_ref):
    hx_ref[...] = _modnorm(x_ref[0], gs_ref[0], sh_ref[0]).astype(BF16)
    cos = cos_ref[...]
    sin = sin_ref[...]
    half = BW // 2

    def rot(p, scale):
        p1, p2 = p[:, :half], p[:, half:]
        return jnp.concatenate([(p1 * cos - p2 * sin) * scale, (p1 * sin + p2 * cos) * scale], axis=-1)

    pq = _dot(hx_ref[...], w_ref[:, 0:BW])
    qkvg_ref[0, :, 0:BW] = rot(pq, 1.0).astype(BF16)
    pk = _dot(hx_ref[...], w_ref[:, BW:2 * BW])
    qkvg_ref[0, :, BW:2 * BW] = rot(pk, RET_HEAD_DIM ** -0.5).astype(BF16)
    pv = _dot(hx_ref[...], w_ref[:, 2 * BW:4 * BW])
    qkvg_ref[0, :, 2 * BW:4 * BW] = pv.astype(BF16)
    pc = _dot(hx_ref[...], w_ref[:, 4 * BW:6 * BW])
    conf_ref[0] = (pc[:, :BW] * _sigmoid(pc[:, BW:])).astype(BF16)
    gz_ref[0] = _dot(hx_ref[...], w_ref[:, 6 * BW:8 * BW]).astype(BF16)
    pf = _dot(hx_ref[...], w_ref[:, 8 * BW:9 * BW])
    rows = pf.shape[0] // FNET_RADIX
    for c in range(BW // LANES):
        pf_ref[c] = pf[:, c * LANES:(c + 1) * LANES]
        for n1 in range(FNET_RADIX):
            ff_ref[0, :, n1 * BW + c * LANES:n1 * BW + (c + 1) * LANES] = (
                pf_ref[c, pl.ds(n1, rows, stride=FNET_RADIX), :].astype(BF16))


def _inproj(x, gs, sh, w_small, cos_t, sin_t):
    b, l, d = x.shape
    tm = min(l, 2 * TOKEN_TILE)
    vec = pl.BlockSpec((1, 1, d), lambda i, j: (i, 0, 0))
    tab = pl.BlockSpec((tm, BW // 2), lambda i, j: (j, 0))

    def out(c):
        return pl.BlockSpec((1, tm, c), lambda i, j: (i, j, 0))

    return pl.pallas_call(
        _inproj_kernel,
        grid=(b, l // tm),
        in_specs=[pl.BlockSpec((1, tm, d), lambda i, j: (i, j, 0)), vec, vec,
                  pl.BlockSpec((d, SMALL_COLS), lambda i, j: (0, 0)), tab, tab],
        out_specs=[out(4 * BW), out(BW), out(2 * BW),
                   pl.BlockSpec((1, tm // FNET_RADIX, FNET_RADIX * BW), lambda i, j: (i, j, 0))],
        out_shape=[jax.ShapeDtypeStruct((b, l, c), BF16) for c in (4 * BW, BW, 2 * BW)]
        + [jax.ShapeDtypeStruct((b, l // FNET_RADIX, FNET_RADIX * BW), BF16)],
        scratch_shapes=[pltpu.VMEM((tm, d), BF16), pltpu.VMEM((BW // LANES, tm, LANES), F32)],
        compiler_params=_cparams("parallel", "parallel"),
        name="inproj",
    )(x, gs, sh, w_small, cos_t, sin_t)


def _mixer_kernel(n_chunks, qkvg_ref, conf_ref, gz_ref, sf0_ref, sb0_ref,
                  mst_ref, wqf_ref, wqb_ref, wkf_ref, wkb_ref, cdf_ref, cdb_ref, bd_ref, avg_ref, gn_ref, hm_ref,
                  cw_ref, cb_ref, cg_ref, cbe_ref, mg_ref, mbe_ref, mws_ref, mbs_ref, mgm_ref,
                  out_ref, cout_ref, mout_ref, sf_ref, sb_ref, o_scr, uf_scr, ub_scr, st_scr, hp_scr):
    unroll = 4 if n_chunks % 4 == 0 else (2 if n_chunks % 2 == 0 else 1)
    l = n_chunks * CHUNK
    hp_scr[0:CONV_PAD, :] = jnp.zeros((CONV_PAD, BW), F32)
    hp_scr[CONV_PAD + l:2 * CONV_PAD + l, :] = jnp.zeros((CONV_PAD, BW), F32)

    def chunk_rows(n):
        return pl.ds(pl.multiple_of(n * CHUNK, CHUNK), CHUNK)

    def state_update(k, wk_ref, v):
        kdec = (k * wk_ref[...]).astype(BF16)
        upd = lax.dot_general(kdec, v, (((0,), (0,)), ((), ())), preferred_element_type=F32)
        return bd_ref[...] * upd

    def intra(n, carry):
        rows = chunk_rows(n)
        q = qkvg_ref[0, rows, 0:BW]
        k = qkvg_ref[0, rows, BW:2 * BW]
        v = qkvg_ref[0, rows, 2 * BW:3 * BW]
        kf = k.astype(F32)
        qst = jnp.concatenate([q * hm_ref[h] for h in range(RET_HEADS)], axis=0)
        s = lax.dot_general(qst, k, (((1,), (1,)), ((), ())), preferred_element_type=F32)
        p = (s * mst_ref[...]).astype(BF16)
        pcat = jnp.concatenate([p[h * CHUNK:(h + 1) * CHUNK] for h in range(RET_HEADS)], axis=1)
        vbd = jnp.concatenate([v * hm_ref[RET_HEADS + h] for h in range(RET_HEADS)], axis=0)
        o_scr[rows, :] = _dot(pcat, vbd)
        uf_scr[n] = state_update(kf, wkf_ref, v)
        ub_scr[n] = state_update(kf, wkb_ref, v)
        mout_ref[0, rows, :] = _gmlp_chunk(gz_ref[0, rows, :], mg_ref, mbe_ref, mws_ref, mbs_ref,
                                           mgm_ref).astype(BF16)
        hp_scr[pl.ds(pl.multiple_of(n * CHUNK + CONV_PAD, SUBLANES), CHUNK), :] = conf_ref[0, rows, :].astype(F32)
        return carry

    lax.fori_loop(0, n_chunks, intra, 0, unroll=unroll)

    sf_ref[0] = sf0_ref[0]
    sb_ref[0] = sb0_ref[0]

    def scan(i, carry):
        st_scr[i, 0:BW, :] = sf_ref[0].astype(BF16)
        sf_ref[0] = cdf_ref[...] * sf_ref[0] + uf_scr[i]
        n = n_chunks - 1 - i
        st_scr[n, BW:2 * BW, :] = sb_ref[0].astype(BF16)
        sb_ref[0] = cdb_ref[...] * sb_ref[0] + ub_scr[n]
        return carry

    lax.fori_loop(0, n_chunks, scan, 0)

    def group_mean(t):
        hi = t.astype(BF16)
        lo = (t - hi.astype(F32)).astype(BF16)
        return _dot(jnp.concatenate([hi, lo], axis=1), avg_ref[...])

    def finish(n, carry):
        rows = chunk_rows(n)
        q = qkvg_ref[0, rows, 0:BW].astype(F32)
        qdec = jnp.concatenate([(q * wqf_ref[...]).astype(BF16), (q * wqb_ref[...]).astype(BF16)], axis=1)
        o = o_scr[rows, :] + _dot(qdec, st_scr[n])
        oc = o - group_mean(o)
        y = oc * lax.rsqrt(group_mean(oc * oc) + EPS) * gn_ref[...]
        g = qkvg_ref[0, rows, 3 * BW:4 * BW].astype(F32)
        out_ref[0, rows, :] = (_silu(g) * y).astype(BF16)
        for t in range(CHUNK // CONV_TILE):
            base = pl.multiple_of(n * CHUNK + t * CONV_TILE, CONV_TILE)
            cout_ref[0, pl.ds(base, CONV_TILE), :] = _conv_tile(hp_scr, base, cw_ref, cb_ref, cg_ref,
                                                                cbe_ref).astype(BF16)
        return carry

    lax.fori_loop(0, n_chunks, finish, 0, unroll=unroll)


def _mixers(qkvg, conf_h, gz, sf0, sb0, tabs, gn, conv_consts, gmlp_consts):
    b, l, _ = qkvg.shape
    n_chunks = l // CHUNK
    st = pl.BlockSpec((1, BW, BW), lambda i: (i, 0, 0))

    def seq(c):
        return pl.BlockSpec((1, l, c), lambda i: (i, 0, 0))

    def full(a):
        return pl.BlockSpec(a.shape, lambda i: (0,) * a.ndim)

    consts = ([tabs[k] for k in ("mst", "wqf", "wqb", "wkf", "wkb", "cdf", "cdb", "bd", "avg")] + [gn, tabs["hm"]]
              + conv_consts + gmlp_consts)
    return pl.pallas_call(
        functools.partial(_mixer_kernel, n_chunks),
        grid=(b,),
        in_specs=[seq(4 * BW), seq(BW), seq(2 * BW), st, st] + [full(a) for a in consts],
        out_specs=[seq(BW), seq(BW), seq(BW), st, st],
        out_shape=[jax.ShapeDtypeStruct((b, l, BW), BF16)] * 3 + [jax.ShapeDtypeStruct((b, BW, BW), F32)] * 2,
        scratch_shapes=[pltpu.VMEM((l, BW), F32),
                        pltpu.VMEM((n_chunks, BW, BW), F32),
                        pltpu.VMEM((n_chunks, BW, BW), F32),
                        pltpu.VMEM((n_chunks, 2 * BW, BW), BF16),
                        pltpu.VMEM((l + 2 * CONV_PAD, BW), F32)],
        compiler_params=_cparams("parallel"),
        name="mixers",
    )(qkvg, conf_h, gz, sf0, sb0, *consts)


def _state_kernel(n_chunks, k_ref, v_ref, wkf_ref, wkb_ref, cdf_ref, cdb_ref, bd_ref, sf_ref, sb_ref):
    sf_ref[0] = jnp.zeros((BW, BW), F32)
    sb_ref[0] = jnp.zeros((BW, BW), F32)

    def increment(n, wk_ref):
        rows = pl.ds(pl.multiple_of(n * CHUNK, CHUNK), CHUNK)
        kdec = (k_ref[0, rows, :].astype(F32) * wk_ref[...]).astype(BF16)
        upd = lax.dot_general(kdec, v_ref[0, rows, :], (((0,), (0,)), ((), ())), preferred_element_type=F32)
        return bd_ref[...] * upd

    def scan(i, carry):
        sf_ref[0] = cdf_ref[...] * sf_ref[0] + increment(i, wkf_ref)
        sb_ref[0] = cdb_ref[...] * sb_ref[0] + increment(n_chunks - 1 - i, wkb_ref)
        return carry

    lax.fori_loop(0, n_chunks, scan, 0)


def _final_states(qkvg, tabs):
    b, l, _ = qkvg.shape
    st = pl.BlockSpec((1, BW, BW), lambda i: (i, 0, 0))

    def full(a):
        return pl.BlockSpec(a.shape, lambda i: (0,) * a.ndim)

    consts = [tabs[k] for k in ("wkf", "wkb", "cdf", "cdb", "bd")]
    return pl.pallas_call(
        functools.partial(_state_kernel, l // CHUNK),
        grid=(b,),
        in_specs=[pl.BlockSpec((1, l, BW), lambda i: (i, 0, 1)),
                  pl.BlockSpec((1, l, BW), lambda i: (i, 0, 2))] + [full(a) for a in consts],
        out_specs=[st, st],
        out_shape=[jax.ShapeDtypeStruct((b, BW, BW), F32)] * 2,
        compiler_params=_cparams("parallel"),
        name="final_states",
    )(qkvg, qkvg, *consts)


def _retention_tables(log_g):
    lf, lb = log_g[0], log_g[1]
    idx = jnp.arange(CHUNK, dtype=F32)
    diff = idx[:, None] - idx[None, :]
    m = (jnp.where(diff > 0, jnp.exp(lf[:, None, None] * jnp.maximum(diff, 0.0)), 0.0)
         + jnp.where(diff < 0, jnp.exp(lb[:, None, None] * jnp.maximum(-diff, 0.0)), 0.0)
         + jnp.where(diff == 0, 2.0, 0.0))
    lane = np.arange(BW)
    hq = (lane % (BW // 2)) // (RET_HEAD_DIM // 2)
    hv = lane // RET_HEAD_DIM
    lfq, lbq = lf[hq][None, :], lb[hq][None, :]
    i = idx[:, None]
    hm = np.concatenate([(hq[None, :] == np.arange(RET_HEADS)[:, None]),
                         (hv[None, :] == np.arange(RET_HEADS)[:, None])], axis=0).astype(np.float32)
    return dict(
        mst=m.reshape(RET_HEADS * CHUNK, CHUNK),
        wqf=jnp.exp(lfq * (i + 1.0)), wqb=jnp.exp(lbq * (CHUNK - i)),
        wkf=jnp.exp(lfq * (CHUNK - 1.0 - i)), wkb=jnp.exp(lbq * i),
        cdf=jnp.broadcast_to(jnp.exp(lf[hq] * CHUNK)[:, None], (BW, BW)),
        cdb=jnp.broadcast_to(jnp.exp(lb[hq] * CHUNK)[:, None], (BW, BW)),
        bd=jnp.asarray((hq[:, None] == hv[None, :]).astype(np.float32)),
        avg=jnp.asarray(np.tile((hv[:, None] == hv[None, :]).astype(np.float32) / RET_HEAD_DIM,
                                (2, 1))).astype(BF16),
        hm=jnp.asarray(np.broadcast_to(hm[:, None, :], (2 * RET_HEADS, CHUNK, BW))).astype(BF16),
    )


CONV_TILE = 128
CONV_PAD = 16


def _conv_tile(hp_ref, base, w_ref, b_ref, g_ref, be_ref):
    first = CONV_PAD - (CONV_KERNEL - 1) // 2
    win = hp_ref[pl.ds(base, CONV_TILE + 2 * CONV_PAD), :]
    acc = jnp.zeros((CONV_TILE, BW), F32) + b_ref[...]
    for s in range(SUBLANES):
        part = None
        for m in range(2 * CONV_PAD // SUBLANES):
            k = SUBLANES * m + s - first
            if 0 <= k < CONV_KERNEL:
                term = win[SUBLANES * m:SUBLANES * m + CONV_TILE + SUBLANES, :] * w_ref[k:k + 1, :]
                part = term if part is None else part + term
        if part is not None:
            acc = acc + part[s:s + CONV_TILE, :]
    return _silu(_layer_norm(acc, g_ref[...], be_ref[...]))


def _conv_consts(w_dw, b_dw, ln_g, ln_b):
    return [w_dw, b_dw.reshape(1, BW), ln_g.reshape(1, BW), ln_b.reshape(1, BW)]


def _gmlp_chunk(z, g_ref, be_ref, ws_ref, bs_ref, gm_ref):
    z = z.astype(F32)
    z = 0.5 * z * (1.0 + lax.erf(z * (2.0 ** -0.5)))
    u, v = z[:, :BW], z[:, BW:]
    v = _layer_norm(v, g_ref[...], be_ref[...]).astype(BF16)
    full = _dot(ws_ref[...], v)
    sv = bs_ref[...]
    for g in range(GROUPS):
        sv = sv + full[g * CHUNK:(g + 1) * CHUNK, :] * gm_ref[g:g + 1, :]
    return u * sv


def _gmlp_consts(ln_g, ln_b, ws, bs):
    gm = np.repeat(np.eye(GROUPS, dtype=np.float32), GROUP_C, axis=1)
    return [ln_g.reshape(1, BW), ln_b.reshape(1, BW), ws.reshape(GROUPS * CHUNK, CHUNK).astype(BF16),
            jnp.repeat(bs.T, GROUP_C, axis=1), jnp.asarray(gm)]


FNET_RADIX = 8


@functools.lru_cache(maxsize=None)
def _dft_tables(l):
    n2 = l // FNET_RADIX
    c = np.arange(GROUP_C, dtype=np.int64)
    angc = 2.0 * np.pi * ((c[:, None] * c[None, :]) % GROUP_C) / GROUP_C
    eye = np.eye(GROUPS)
    wcs = np.concatenate([np.kron(eye, np.cos(angc)), np.kron(eye, np.sin(angc))], axis=1)
    n = np.arange(n2, dtype=np.int64)
    ang = 2.0 * np.pi * ((n[:, None] * n[None, :]) % n2) / n2
    cc, ss = np.cos(ang), np.sin(ang)
    m1 = np.block([[cc, ss], [-ss, cc]])
    angt = 2.0 * np.pi * (n[:, None] * np.arange(FNET_RADIX)[None, :]) / l
    tc = np.repeat(np.cos(angt), BW, axis=1)
    ts = np.repeat(np.sin(angt), BW, axis=1)
    return tuple(t.astype(np.float32) for t in (wcs, m1, tc, ts))


def _fnet_kernel(n2, scale, f_ref, wcs_ref, m1_ref, tc_ref, ts_ref, out_ref, ab_ref, zr_ref, zi_ref):
    for i in range(FNET_RADIX):
        cols = slice(i * BW, (i + 1) * BW)
        cs = _dot(f_ref[0, :, cols], wcs_ref[...])
        ab_ref[0:n2, cols] = cs[:, :BW].astype(BF16)
        ab_ref[n2:2 * n2, cols] = (-cs[:, BW:]).astype(BF16)
    for i in range(FNET_RADIX):
        cols = slice(i * BW, (i + 1) * BW)
        y = _dot(m1_ref[...], ab_ref[:, cols])
        yr, yi = y[:n2], y[n2:]
        if i == 0:
            zr_ref[:, cols] = yr
            zi_ref[:, cols] = yi
        else:
            tc, ts = tc_ref[:, cols], ts_ref[:, cols]
            zr_ref[:, cols] = yr * tc + yi * ts
            zi_ref[:, cols] = yi * tc - yr * ts
    step = min(n2, 64)
    half = 0.5 ** 0.5
    for t in range(n2 // step):
        rows = slice(t * step, (t + 1) * step)
        zr = [zr_ref[rows, i * BW:(i + 1) * BW] for i in range(FNET_RADIX)]
        zi = [zi_ref[rows, i * BW:(i + 1) * BW] for i in range(FNET_RADIX)]
        a, b, c, d = zr[0] + zr[4], zr[0] - zr[4], zr[2] + zr[6], zi[2] - zi[6]
        even = (a + c, b + d, a - c, b - d)
        pr, mr = zr[1] + zr[5], zr[1] - zr[5]
        qr, nr = zr[3] + zr[7], zr[3] - zr[7]
        pim, mi = zi[1] + zi[5], zi[1] - zi[5]
        qi, ni = zi[3] + zi[7], zi[3] - zi[7]
        odd_r = (pr + qr, mr + ni, pr - qr, mr - ni)
        odd_i = (pim + qi, mi - nr, pim - qi, mi + nr)
        turned = (odd_r[0], (odd_r[1] + odd_i[1]) * half, odd_i[2], (odd_i[3] - odd_r[3]) * half)
        for k in range(FNET_RADIX // 2):
            for k1, val in ((k, even[k] + turned[k]), (k + FNET_RADIX // 2, even[k] - turned[k])):
                out_ref[0, k1 * n2 + t * step:k1 * n2 + (t + 1) * step, :] = (val * scale).astype(BF16)


def _fnet(f):
    b, n2, _ = f.shape
    l = n2 * FNET_RADIX
    wcs, m1, tc, ts = _dft_tables(l)
    scale = float(1.0 / np.sqrt(l * GROUP_C))

    def full(a):
        return pl.BlockSpec(a.shape, lambda i: (0,) * a.ndim)

    consts = [jnp.asarray(wcs).astype(BF16), jnp.asarray(m1).astype(BF16), jnp.asarray(tc), jnp.asarray(ts)]
    return pl.pallas_call(
        functools.partial(_fnet_kernel, n2, scale),
        grid=(b,),
        in_specs=[pl.BlockSpec((1, n2, FNET_RADIX * BW), lambda i: (i, 0, 0))] + [full(a) for a in consts],
        out_specs=pl.BlockSpec((1, l, BW), lambda i: (i, 0, 0)),
        out_shape=jax.ShapeDtypeStruct((b, l, BW), BF16),
        scratch_shapes=[pltpu.VMEM((2 * n2, FNET_RADIX * BW), BF16),
                        pltpu.VMEM((n2, FNET_RADIX * BW), F32),
                        pltpu.VMEM((n2, FNET_RADIX * BW), F32)],
        compiler_params=_cparams("parallel"),
        name="fnet",
    )(f, *consts)


def _merge_kernel(x_ref, gs_ref, sh_ref, ga_ref, p0_ref, p1_ref, p2_ref, p3_ref,
                  wg_ref, bg_ref, wo_ref, wout_ref, out_ref, hx_ref, m_ref):
    x = x_ref[0]
    hx_ref[...] = _modnorm(x, gs_ref[0], sh_ref[0]).astype(BF16)
    for i, p_ref in enumerate((p0_ref, p1_ref, p2_ref, p3_ref)):
        cols = slice(i * D_MODEL, (i + 1) * D_MODEL)
        gate = _sigmoid(_dot(hx_ref[...], wg_ref[:, cols]) + bg_ref[:, cols])
        term = gate * _dot(p_ref[0], wo_ref[i])
        if i == 0:
            m_ref[...] = term
        else:
            m_ref[...] += term
    out_ref[0] = x + ga_ref[0] * _dot(m_ref[...].astype(BF16), wout_ref[...])


def _merge(x, gs, sh, ga, pres, w_gate, b_gate, w_bo, w_out):
    b, l, d = x.shape
    tm = min(l, TOKEN_TILE)
    vec = pl.BlockSpec((1, 1, d), lambda i, j: (i, 0, 0))
    pre = pl.BlockSpec((1, tm, BW), lambda i, j: (i, j, 0))
    return pl.pallas_call(
        _merge_kernel,
        grid=(b, l // tm),
        in_specs=[pl.BlockSpec((1, tm, d), lambda i, j: (i, j, 0)), vec, vec, vec, pre, pre, pre, pre,
                  pl.BlockSpec((d, 4 * d), lambda i, j: (0, 0)),
                  pl.BlockSpec((1, 4 * d), lambda i, j: (0, 0)),
                  pl.BlockSpec((4, BW, d), lambda i, j: (0, 0, 0)),
                  pl.BlockSpec((d, d), lambda i, j: (0, 0))],
        out_specs=pl.BlockSpec((1, tm, d), lambda i, j: (i, j, 0)),
        out_shape=jax.ShapeDtypeStruct((b, l, d), F32),
        scratch_shapes=[pltpu.VMEM((tm, d), BF16), pltpu.VMEM((tm, d), F32)],
        compiler_params=_cparams("parallel", "parallel"),
        name="merge",
    )(x, gs, sh, ga, *pres, w_gate, b_gate.reshape(1, 4 * d), w_bo, w_out)


FFN_TC = 256
FFN_ROWS = 256
FFN_SUB = 128


def _ffn_pad(grid_w, row_conv):
    return (grid_w if row_conv else 0) + SUBLANES


def _ffn_kernel(l, grid_w, row_conv, final_norm, x_ref, gs_ref, sh_ref, ga_ref, wa_ref, wb_ref, dw_ref,
                db_ref, wd_ref, gf_ref, out_ref, hx_ref, ac_ref, al_ref, ar_ref, b_ref, h_ref, wds_ref):
    j = pl.program_id(1)
    pad = _ffn_pad(grid_w, row_conv)
    n_tiles = l // FFN_ROWS

    @pl.when(j == 0)
    def _():
        for t in range(n_tiles):
            rows = slice(t * FFN_ROWS, (t + 1) * FFN_ROWS)
            x = x_ref[0, rows, :]
            hx_ref[rows, :] = _modnorm(x, gs_ref[0], sh_ref[0]).astype(BF16)
            out_ref[0, rows, :] = x
        for ref in (ac_ref, al_ref, ar_ref):
            ref[0:pad, :] = jnp.zeros((pad, FFN_TC), F32)
            ref[pad + l:2 * pad + l, :] = jnp.zeros((pad, FFN_TC), F32)

    wds_ref[...] = (wd_ref[...].astype(F32) * ga_ref[0]).astype(BF16)

    col = lax.broadcasted_iota(jnp.int32, (FFN_ROWS, FFN_TC), 0) % grid_w
    not_first = col != 0
    not_last = col != grid_w - 1

    step = min(l, TOKEN_TILE)
    for t in range(l // step):
        rows = slice(t * step, (t + 1) * step)
        ac_ref[pad + t * step:pad + (t + 1) * step, :] = _dot(hx_ref[rows, :], wa_ref[...])
        b_ref[rows, :] = _dot(hx_ref[rows, :], wb_ref[...])

    for t in range(n_tiles):
        lo = pad + t * FFN_ROWS
        al_ref[lo:lo + FFN_ROWS, :] = jnp.where(not_first, ac_ref[lo - 1:lo - 1 + FFN_ROWS, :], 0.0)
        ar_ref[lo:lo + FFN_ROWS, :] = jnp.where(not_last, ac_ref[lo + 1:lo + 1 + FFN_ROWS, :], 0.0)

    for t in range(l // FFN_SUB):
        acc = jnp.zeros((FFN_SUB, FFN_TC), F32) + db_ref[...]
        for dr in ((-1, 0, 1) if row_conv else (0,)):
            off = pad + t * FFN_SUB + dr * grid_w
            tap = 3 * (dr + 1)
            acc = acc + al_ref[off:off + FFN_SUB, :] * dw_ref[tap:tap + 1, :]
            acc = acc + ac_ref[off:off + FFN_SUB, :] * dw_ref[tap + 1:tap + 2, :]
            acc = acc + ar_ref[off:off + FFN_SUB, :] * dw_ref[tap + 2:tap + 3, :]
        rows = slice(t * FFN_SUB, (t + 1) * FFN_SUB)
        h_ref[rows, :] = (_silu(acc) * b_ref[rows, :]).astype(BF16)

    for t in range(l // step):
        rows = slice(t * step, (t + 1) * step)
        out_ref[0, rows, :] += _dot(h_ref[rows, :], wds_ref[...])

    if final_norm:
        @pl.when(j == pl.num_programs(1) - 1)
        def _():
            for t in range(l // FFN_ROWS):
                rows = slice(t * FFN_ROWS, (t + 1) * FFN_ROWS)
                y = out_ref[0, rows, :]
                out_ref[0, rows, :] = y * lax.rsqrt(jnp.mean(y * y, axis=-1, keepdims=True) + EPS) * gf_ref[...]


def _conv_ffn(x, gs, sh, ga, w_up, dw, db, w_down, g_final, grid_w, row_conv, final_norm):
    b, l, d = x.shape
    nj = D_FF // FFN_TC
    pad = _ffn_pad(grid_w, row_conv)
    vec = pl.BlockSpec((1, 1, d), lambda i, j: (i, 0, 0))
    return pl.pallas_call(
        functools.partial(_ffn_kernel, l, grid_w, row_conv, final_norm),
        grid=(b, nj),
        in_specs=[pl.BlockSpec((1, l, d), lambda i, j: (i, 0, 0)), vec, vec, vec,
                  pl.BlockSpec((d, FFN_TC), lambda i, j: (0, j)),
                  pl.BlockSpec((d, FFN_TC), lambda i, j: (0, j + nj)),
                  pl.BlockSpec((9, FFN_TC), lambda i, j: (0, j)),
                  pl.BlockSpec((1, FFN_TC), lambda i, j: (0, j)),
                  pl.BlockSpec((FFN_TC, d), lambda i, j: (j, 0)),
                  pl.BlockSpec((1, d), lambda i, j: (0, 0))],
        out_specs=pl.BlockSpec((1, l, d), lambda i, j: (i, 0, 0)),
        out_shape=jax.ShapeDtypeStruct((b, l, d), F32),
        scratch_shapes=([pltpu.VMEM((l, d), BF16)] + [pltpu.VMEM((l + 2 * pad, FFN_TC), F32)] * 3
                        + [pltpu.VMEM((l, FFN_TC), F32), pltpu.VMEM((l, FFN_TC), BF16),
                           pltpu.VMEM((FFN_TC, d), BF16)]),
        compiler_params=_cparams("parallel", "arbitrary"),
        name="conv_ffn",
    )(x, gs, sh, ga, w_up, w_up, dw.reshape(9, D_FF), db.reshape(1, D_FF), w_down, g_final.reshape(1, d))


def _rope_tables(n_pos):
    half = RET_HEAD_DIM // 2
    inv_freq = ROPE_BASE ** (-jnp.arange(half, dtype=F32) / half)
    ang = jnp.arange(n_pos, dtype=jnp.int32).astype(F32)[:, None] * inv_freq[None, :]
    return jnp.tile(jnp.cos(ang), (1, RET_HEADS)), jnp.tile(jnp.sin(ang), (1, RET_HEADS))


def _split_half_perm():
    half = RET_HEAD_DIM // 2
    t, h, i = np.meshgrid(np.arange(2), np.arange(RET_HEADS), np.arange(half), indexing="ij")
    return (h * RET_HEAD_DIM + t * half + i).reshape(-1)


def kernel(x, c, ctx, c_ctx, w_ada, b_ada, g_norm1, g_norm2, w_in, b_gate, ret_decay, ret_gn, w_ret_o, conv_dw, conv_db, conv_ln_g, conv_ln_b, w_conv_o, gmlp_ln_g, gmlp_ln_b, gmlp_ws, gmlp_bs, w_gmlp_o, w_fnet_o, w_out, w_ffn_up, ffn_dw, ffn_db, w_ffn_down, g_final):
    bsz, seq, d = x.shape
    n_ctx = ctx.shape[1]
    pad_rows = -(bsz + 1) % SUBLANES
    cc = jnp.concatenate([c, c_ctx[None, :], jnp.zeros((pad_rows, d), F32)], axis=0)
    mods = _ada(cc, w_ada, b_ada)
    cos_t, sin_t = _rope_tables(n_ctx + seq)
    perm = _split_half_perm()
    zero_state = jnp.zeros((bsz, BW, BW), F32)
    xc = ctx

    def mixers(stream, l_idx, gs, sh, ga, cos_p, sin_p, sf0, sb0, tabs, want_out):
        w_small = jnp.concatenate([w_in[l_idx, :, :BW][:, perm], w_in[l_idx, :, BW:2 * BW][:, perm],
                                   w_in[l_idx, :, 2 * BW:SMALL_COLS]], axis=1).astype(BF16)
        qkvg, conf_h, gz, ff = _inproj(stream, gs, sh, w_small, cos_p, sin_p)
        if not want_out:
            return (None,) + tuple(_final_states(qkvg, tabs))
        ret_pre, conf_pre, gm_pre, sf, sb = _mixers(
            qkvg, conf_h, gz, sf0, sb0, tabs, ret_gn[l_idx].reshape(1, BW),
            _conv_consts(conv_dw[l_idx], conv_db[l_idx], conv_ln_g[l_idx], conv_ln_b[l_idx]),
            _gmlp_consts(gmlp_ln_g[l_idx], gmlp_ln_b[l_idx], gmlp_ws[l_idx], gmlp_bs[l_idx]))
        fn_pre = _fnet(ff)
        w_bo = jnp.stack([w_ret_o[l_idx], w_conv_o[l_idx], w_gmlp_o[l_idx], w_fnet_o[l_idx]]).astype(BF16)
        new = _merge(stream, gs, sh, ga, (ret_pre, conf_pre, gm_pre, fn_pre),
                     w_in[l_idx, :, GATE_OFF:].astype(BF16), b_gate[l_idx], w_bo, w_out[l_idx].astype(BF16))
        return new, sf, sb

    for l_idx in range(DEPTH):
        last = l_idx == DEPTH - 1
        lat = [m[:, None, :] for m in jnp.split(mods[l_idx, :bsz], 6, axis=-1)]
        cm = [jnp.broadcast_to(m[None, None, :], (bsz, 1, d)) for m in jnp.split(mods[l_idx, bsz], 6, axis=-1)]
        g1 = g_norm1[l_idx][None, None, :]
        g2 = g_norm2[l_idx][None, None, :]
        tabs = _retention_tables(jax.nn.log_sigmoid(ret_decay[l_idx].astype(F32)))
        w_up = w_ffn_up[l_idx].astype(BF16)
        w_down = w_ffn_down[l_idx].astype(BF16)

        xc_mix, s_f, s_b = mixers(xc, l_idx, g1 * (1 + cm[1]), cm[0], cm[2], cos_t[:n_ctx], sin_t[:n_ctx],
                                  zero_state, zero_state, tabs, not last)
        x, _, _ = mixers(x, l_idx, g1 * (1 + lat[1]), lat[0], lat[2], cos_t[n_ctx:], sin_t[n_ctx:],
                         s_f, s_b, tabs, True)
        x = _conv_ffn(x, g2 * (1 + lat[4]), lat[3], lat[5], w_up, ffn_dw[l_idx], ffn_db[l_idx], w_down,
                      g_final, GRID_W, True, last)
        if not last:
            fold = max(1, seq // n_ctx)
            while bsz % fold:
                fold //= 2
            xc = _conv_ffn(xc_mix.reshape(bsz // fold, fold * n_ctx, d), (g2 * (1 + cm[4]))[:bsz // fold],
                           cm[3][:bsz // fold], cm[5][:bsz // fold], w_up, ffn_dw[l_idx], ffn_db[l_idx],
                           w_down, g_final, n_ctx, False, False).reshape(bsz, n_ctx, d)
    return x
```
